```python
import jax
import jax.numpy as jnp
from jax import lax
import numpy as np


D_MODEL = 1024
BATCH = 4
SEQ = 8192
DEPTH = 4

GRID_W = 64
CTX_LEN = 256
EPS = 1e-6
F_MIN = 1e-6
F32 = jnp.float32
N_MOD = 9
D_FF = 2816
HG_HEADS = 4
HG_DK = 64
HG_DV = 64
HG_CHUNK = 64
MLA_HEADS = 8
MLA_Q_RANK = 384
MLA_KV_RANK = 256
MLA_NOPE = 64
MLA_ROPE = 32
MLA_V = 64
MLA_QK = MLA_NOPE + MLA_ROPE
ATTN_BLOCK = 128
ROPE_BASE = 10000.0
POOL_WINDOWS = (2, 4, 8, 16)
POOL_CH = 64
HG_K = HG_HEADS * HG_DK
HG_WIDTH = HG_HEADS * HG_DV
MLA_WIDTH = MLA_HEADS * MLA_V
POOL_WIDTH = len(POOL_WINDOWS) * POOL_CH
MIX_WIDTH = HG_WIDTH + MLA_WIDTH + POOL_WIDTH
IN_SPLITS = (HG_K, HG_K, HG_K, HG_WIDTH, HG_WIDTH, MLA_Q_RANK, MLA_KV_RANK, MLA_ROPE, POOL_WIDTH)
IN_WIDTH = sum(IN_SPLITS)

kernel_name = 'hybrid_hgrn2_mla_pool_macaron_dit'


def rms_norm(x, gain):
    xf = x.astype(F32)
    y = xf * lax.rsqrt(jnp.mean(xf * xf, axis=-1, keepdims=True) + EPS)
    return (y * gain.astype(F32)).astype(x.dtype)


def modulated_rms_norm(x, shift, scale):
    xf = x.astype(F32)
    y = xf * lax.rsqrt(jnp.mean(xf * xf, axis=-1, keepdims=True) + EPS)
    return (y * (1.0 + scale.astype(F32)) + shift.astype(F32)).astype(x.dtype)


def swiglu(h, w_in, w_out):
    g, u = jnp.split(h @ w_in, 2, axis=-1)
    return (jax.nn.silu(g) * u) @ w_out


def split_cols(p):
    return jnp.split(p, [int(o) for o in np.cumsum(IN_SPLITS)[:-1]], axis=-1)


def axial_rope(rows):
    row = jnp.repeat(jnp.arange(rows), GRID_W).astype(F32)
    col = jnp.tile(jnp.arange(GRID_W), rows).astype(F32)
    n_freq = MLA_ROPE // 4
    inv_freq = ROPE_BASE ** (-jnp.arange(n_freq, dtype=F32) / n_freq)
    ang = jnp.concatenate([row[:, None] * inv_freq, col[:, None] * inv_freq], axis=-1)
    return jnp.cos(ang)[:, None, :], jnp.sin(ang)[:, None, :]


def rotate_tail(x, cos, sin):
    nope, pe = x[..., :MLA_NOPE], x[..., MLA_NOPE:]
    half = MLA_ROPE // 2
    x1, x2 = pe[..., :half].astype(F32), pe[..., half:].astype(F32)
    rot = jnp.concatenate([x1 * cos - x2 * sin, x1 * sin + x2 * cos], axis=-1).astype(x.dtype)
    return jnp.concatenate([nope, rot], axis=-1)


def to_heads(a, n_heads):
    b, t, _ = a.shape
    return a.reshape(b, t, n_heads, -1).transpose(0, 2, 1, 3)


def hgrn2_gates(z, lb):
    zf = z.astype(F32)
    f = lb + (1.0 - lb) * jax.nn.sigmoid(zf)
    log_f = jnp.log(jnp.clip(f, F_MIN, 1.0))
    k = 1.0 - f
    return log_f, k


def hgrn2_inputs(q_z, f_fwd_z, f_bwd_z, i_z, lb_fwd, lb_bwd):
    q = to_heads(jax.nn.silu(q_z.astype(F32)), HG_HEADS)
    v = to_heads(i_z.astype(F32), HG_HEADS)
    lf_f, k_f = hgrn2_gates(f_fwd_z, lb_fwd)
    lf_b, k_b = hgrn2_gates(f_bwd_z, lb_bwd)
    return (q, v, to_heads(lf_f, HG_HEADS), to_heads(k_f, HG_HEADS),
            to_heads(lf_b, HG_HEADS), to_heads(k_b, HG_HEADS))


def gla_chunk_scan(q, k, v, log_f, s0):
    b_, h_, t_, _ = q.shape
    dv = v.shape[-1]
    n = t_ // HG_CHUNK

    def chunks(a):
        return jnp.moveaxis(a.reshape(b_, h_, n, HG_CHUNK, a.shape[-1]), 2, 0)

    incl = jnp.tril(jnp.ones((HG_CHUNK, HG_CHUNK), dtype=bool))[:, :, None]

    def step(s, blk):
        qc, kc, vc, lfc = blk
        b = jnp.cumsum(lfc, axis=2)
        o = jnp.einsum('bhck,bhkv->bhcv', qc * jnp.exp(b), s)
        rel = jnp.where(incl, b[:, :, :, None, :] - b[:, :, None, :, :], 0.0)
        decay = jnp.where(incl, jnp.exp(rel), 0.0)
        att = jnp.einsum('bhck,bhcsk,bhsk->bhcs', qc, decay, kc)
        o = o + jnp.einsum('bhcs,bhsv->bhcv', att, vc)
        b_end = b[:, :, -1:, :]
        s = jnp.exp(b_end[:, :, 0, :, None]) * s + jnp.einsum('bhsk,bhsv->bhkv', kc * jnp.exp(b_end - b), vc)
        return s, o

    s_fin, o = lax.scan(step, s0, (chunks(q), chunks(k), chunks(v), chunks(log_f)))
    return jnp.moveaxis(o, 0, 2).reshape(b_, h_, t_, dv), s_fin


def hgrn2_bidir(q, v, lf_f, k_f, lf_b, k_b, s_f0, s_b0):
    o_f, s_f = gla_chunk_scan(q, k_f, v, lf_f, s_f0)
    rev = lambda a: jnp.flip(a, axis=2)
    o_b, s_b = gla_chunk_scan(rev(q), rev(k_b), rev(v), rev(lf_b), s_b0)
    return o_f + rev(o_b), s_f, s_b


def hgrn2_readout(o, g_z, gain, dtype):
    b_, h_, t_, dv = o.shape
    o = o * lax.rsqrt(jnp.mean(o * o, axis=-1, keepdims=True) + EPS) * gain.astype(F32)
    o = o.transpose(0, 2, 1, 3).reshape(b_, t_, h_ * dv)
    return (o * jax.nn.silu(g_z.astype(F32))).astype(dtype)


def mla_queries(cq, q_a_gain, w_uq, q_gain, rope):
    b_, t_, _ = cq.shape
    q = (rms_norm(cq, q_a_gain) @ w_uq).reshape(b_, t_, MLA_HEADS, MLA_QK)
    q = rms_norm(q, q_gain)
    if rope is not None:
        q = rotate_tail(q, rope[0], rope[1])
    return q


def mla_keys_values(ckv, kpe, kv_a_gain, w_ukv, k_gain, rope):
    b_, t_, _ = ckv.shape
    kv = (rms_norm(ckv, kv_a_gain) @ w_ukv).reshape(b_, t_, MLA_HEADS, MLA_NOPE + MLA_V)
    k_nope, v = kv[..., :MLA_NOPE], kv[..., MLA_NOPE:]
    k_pe = jnp.broadcast_to(kpe[:, :, None, :], (b_, t_, MLA_HEADS, MLA_ROPE)).astype(k_nope.dtype)
    k = rms_norm(jnp.concatenate([k_nope, k_pe], axis=-1), k_gain)
    if rope is not None:
        k = rotate_tail(k, rope[0], rope[1])
    return k, v


def block_attention(q, k, v):
    b_, t_, h_, dq = q.shape
    n = t_ // ATTN_BLOCK
    scale = dq ** -0.5
    qb = jnp.moveaxis(q.reshape(b_, n, ATTN_BLOCK, h_, dq), 1, 0)

    def attend(qblk):
        s = jnp.einsum('bqhd,bkhd->bhqk', qblk, k).astype(F32) * scale
        p = jax.nn.softmax(s, axis=-1).astype(v.dtype)
        return jnp.einsum('bhqk,bkhd->bqhd', p, v)

    o = lax.map(attend, qb)
    return jnp.moveaxis(o, 0, 1).reshape(b_, t_, h_ * v.shape[-1])


def multiscale_pool(u, w_pool, pool_scale):
    b_, t_, _ = u.shape
    n_g = len(POOL_WINDOWS)
    ug = u.reshape(b_, t_, n_g, POOL_CH).astype(F32)
    csum = jnp.concatenate([jnp.zeros((b_, 1, n_g, POOL_CH), F32), jnp.cumsum(ug, axis=1)], axis=1)
    pos = jnp.arange(t_)
    means = []
    for g, w in enumerate(POOL_WINDOWS):
        lo = jnp.clip(pos - w // 2, 0, t_ - 1)
        hi = jnp.clip(pos + w - 1 - w // 2, 0, t_ - 1)
        cs = csum[:, :, g]
        cnt = (hi - lo + 1).astype(F32)[None, :, None]
        means.append((cs[:, hi + 1] - cs[:, lo]) / cnt)
    pooled = (jnp.stack(means, axis=2) - ug).astype(u.dtype)
    y = jnp.einsum('btgc,gcd->btgd', pooled, w_pool).reshape(b_, t_, n_g * POOL_CH)
    return y * pool_scale


def token_mixers(parts_l, parts_c, lb_fwd, lb_bwd, hg_gain, q_a_gain, w_uq, kv_a_gain, w_ukv,
                 q_gain, k_gain, w_pool, pool_scale, rope, need_ctx_out):
    q_l, ff_l, fb_l, i_l, g_l, cq_l, ckv_l, kpe_l, pool_l = parts_l
    q_c, ff_c, fb_c, i_c, g_c, cq_c, ckv_c, kpe_c, pool_c = parts_c
    dtype = q_l.dtype
    b_ = q_l.shape[0]
    s0 = jnp.zeros((b_, HG_HEADS, HG_DK, HG_DV), F32)
    o_c, s_f, s_b = hgrn2_bidir(*hgrn2_inputs(q_c, ff_c, fb_c, i_c, lb_fwd, lb_bwd), s0, s0)
    o_l, _, _ = hgrn2_bidir(*hgrn2_inputs(q_l, ff_l, fb_l, i_l, lb_fwd, lb_bwd), s_f, s_b)
    hg_out = hgrn2_readout(o_l, g_l, hg_gain, dtype)
    k_c, v_c = mla_keys_values(ckv_c, kpe_c, kv_a_gain, w_ukv, k_gain, None)
    k_l, v_l = mla_keys_values(ckv_l, kpe_l, kv_a_gain, w_ukv, k_gain, rope)
    q_lat = mla_queries(cq_l, q_a_gain, w_uq, q_gain, rope)
    att_out = block_attention(q_lat, jnp.concatenate([k_l, k_c], axis=1), jnp.concatenate([v_l, v_c], axis=1))
    pool_out = multiscale_pool(pool_l, w_pool, pool_scale)
    mix_l = jnp.concatenate([hg_out, att_out.astype(dtype), pool_out.astype(dtype)], axis=-1)
    if not need_ctx_out:
        return mix_l, None
    hg_c = hgrn2_readout(o_c, g_c, hg_gain, dtype)
    att_c = block_attention(mla_queries(cq_c, q_a_gain, w_uq, q_gain, None), k_c, v_c)
    pool_cx = multiscale_pool(pool_c, w_pool, pool_scale)
    mix_c = jnp.concatenate([hg_c, att_c.astype(dtype), pool_cx.astype(dtype)], axis=-1)
    return mix_l, mix_c


def setup_inputs(seed: int = 0) -> dict:
    key = jax.random.key(seed)
    ks = jax.random.split(key, 22)
    nrm = lambda k, shape, s: jax.random.normal(k, shape, F32) * s
    gain = lambda k, shape: 1.0 + 0.1 * jax.random.normal(k, shape, F32)
    return {
        'x': nrm(ks[0], (BATCH, SEQ, D_MODEL), 1.0),
        'c': nrm(ks[1], (BATCH, D_MODEL), 1.0),
        'ctx': nrm(ks[2], (BATCH, CTX_LEN, D_MODEL), 1.0),
        'c_ctx': nrm(ks[3], (D_MODEL,), 1.0),
        'w_mod': nrm(ks[4], (DEPTH, D_MODEL, N_MOD * D_MODEL), 0.5 * D_MODEL ** -0.5),
        'b_mod': nrm(ks[5], (DEPTH, N_MOD * D_MODEL), 0.01),
        'ffn1_w_in': nrm(ks[6], (DEPTH, D_MODEL, 2 * D_FF), D_MODEL ** -0.5),
        'ffn1_w_out': nrm(ks[7], (DEPTH, D_FF, D_MODEL), D_FF ** -0.5),
        'w_in': nrm(ks[8], (DEPTH, D_MODEL, IN_WIDTH), D_MODEL ** -0.5),
        'w_out': nrm(ks[9], (DEPTH, MIX_WIDTH, D_MODEL), MIX_WIDTH ** -0.5),
        'hg_lb_logits': nrm(ks[10], (DEPTH, 2, HG_K), 0.1),
        'hg_out_gain': gain(ks[11], (DEPTH, HG_DV)),
        'mla_q_a_gain': gain(ks[12], (DEPTH, MLA_Q_RANK)),
        'mla_w_uq': nrm(ks[13], (DEPTH, MLA_Q_RANK, MLA_HEADS * MLA_QK), MLA_Q_RANK ** -0.5),
        'mla_kv_a_gain': gain(ks[14], (DEPTH, MLA_KV_RANK)),
        'mla_w_ukv': nrm(ks[15], (DEPTH, MLA_KV_RANK, MLA_HEADS * (MLA_NOPE + MLA_V)), MLA_KV_RANK ** -0.5),
        'mla_q_gain': gain(ks[16], (DEPTH, MLA_QK)),
        'mla_k_gain': gain(ks[17], (DEPTH, MLA_QK)),
        'pool_w': nrm(ks[18], (DEPTH, len(POOL_WINDOWS), POOL_CH, POOL_CH), POOL_CH ** -0.5),
        'pool_scale': gain(ks[19], (DEPTH, POOL_WIDTH)),
        'ffn2_w_in': nrm(ks[20], (DEPTH, D_MODEL, 2 * D_FF), D_MODEL ** -0.5),
        'ffn2_w_out': nrm(ks[21], (DEPTH, D_FF, D_MODEL), D_FF ** -0.5),
    }


def reference(x, c, ctx, c_ctx, w_mod, b_mod, ffn1_w_in, ffn1_w_out, w_in, w_out, hg_lb_logits,
              hg_out_gain, mla_q_a_gain, mla_w_uq, mla_kv_a_gain, mla_w_ukv, mla_q_gain, mla_k_gain,
              pool_w, pool_scale, ffn2_w_in, ffn2_w_out):
    b_, n_lat, d_ = x.shape
    rows = n_lat // GRID_W
    rope = axial_rope(rows)
    p = jax.nn.softmax(hg_lb_logits.astype(F32), axis=0)
    lower_bounds = jnp.cumsum(p, axis=0) - p[0]
    c_act = jax.nn.silu(c)
    cc_act = jax.nn.silu(c_ctx)
    xl, xc = x, ctx
    for l in range(DEPTH):
        need_ctx_out = l < DEPTH - 1
        mod_l = (c_act @ w_mod[l] + b_mod[l]).reshape(b_, N_MOD, 1, d_)
        mod_c = (cc_act @ w_mod[l] + b_mod[l]).reshape(N_MOD, 1, 1, d_)
        ml = [mod_l[:, j] for j in range(N_MOD)]
        mc = [mod_c[j] for j in range(N_MOD)]
        xl = xl + ml[2] * (0.5 * swiglu(modulated_rms_norm(xl, ml[0], ml[1]), ffn1_w_in[l], ffn1_w_out[l]))
        xc = xc + mc[2] * (0.5 * swiglu(modulated_rms_norm(xc, mc[0], mc[1]), ffn1_w_in[l], ffn1_w_out[l]))
        parts_l = split_cols(modulated_rms_norm(xl, ml[3], ml[4]) @ w_in[l])
        parts_c = split_cols(modulated_rms_norm(xc, mc[3], mc[4]) @ w_in[l])
        mix_l, mix_c = token_mixers(parts_l, parts_c, lower_bounds[l, 0], lower_bounds[l, 1], hg_out_gain[l],
                                    mla_q_a_gain[l], mla_w_uq[l], mla_kv_a_gain[l], mla_w_ukv[l],
                                    mla_q_gain[l], mla_k_gain[l], pool_w[l], pool_scale[l], rope, need_ctx_out)
        xl = xl + ml[5] * (mix_l @ w_out[l])
        xl = xl + ml[8] * (0.5 * swiglu(modulated_rms_norm(xl, ml[6], ml[7]), ffn2_w_in[l], ffn2_w_out[l]))
        if need_ctx_out:
            xc = xc + mc[5] * (mix_c @ w_out[l])
            xc = xc + mc[8] * (0.5 * swiglu(modulated_rms_norm(xc, mc[6], mc[7]), ffn2_w_in[l], ffn2_w_out[l]))
    return xl
```

```python
import functools

import numpy as np
import jax
import jax.numpy as jnp
from jax import lax
from jax.experimental import pallas as pl
from jax.experimental.pallas import tpu as pltpu

F32 = jnp.float32
BF16 = jnp.bfloat16
EPS = 1e-6
F_MIN = 1e-6
N_MOD = 9
GRID_W = 64
ROPE_BASE = 10000.0
HG_HEADS = 4
HG_DK = 64
HG_DV = 64
HG_W = HG_HEADS * HG_DK
HG_CHUNK = 64
HG_SUB = 16
MLA_HEADS = 8
MLA_Q_RANK = 384
MLA_KV_RANK = 256
MLA_NOPE = 64
MLA_ROPE = 32
MLA_V = 64
MLA_QK = MLA_NOPE + MLA_ROPE
HEAD_PAD = 128
POOL_WINDOWS = (2, 4, 8, 16)
POOL_CH = 64
POOL_W = len(POOL_WINDOWS) * POOL_CH
POOL_HALO = 8
TOKEN_TILE = 256
FFN_TILE = 512
FFN_CHUNK = 256
ATTN_KEY_TILE = 256
VMEM_LIMIT = 56 * 1024 * 1024

C_Q, C_FF, C_FB, C_I, C_G = 0, 256, 512, 768, 1024
C_CQ = 1280
C_CKV = C_CQ + MLA_Q_RANK
C_POOL = C_CKV + MLA_KV_RANK
C_KPE = C_POOL + POOL_W
IN_PAD = C_KPE + HEAD_PAD


def _dot(a, b):
    return jnp.dot(a, b, preferred_element_type=F32)


def _dot_nt(a, b):
    return lax.dot_general(a, b, (((1,), (1,)), ((), ())), preferred_element_type=F32)


def _dot_tn(a, b):
    return lax.dot_general(a, b, (((0,), (0,)), ((), ())), preferred_element_type=F32)


def _sigmoid(x):
    return 1.0 / (1.0 + jnp.exp(-x))


def _silu(x):
    return x * _sigmoid(x)


def _split3(x):
    hi = x.astype(BF16)
    r = x - hi.astype(F32)
    mid = r.astype(BF16)
    lo = (r - mid.astype(F32)).astype(BF16)
    return hi, mid, lo


def _exact_dot_left(m_bf16, x):
    hi, mid, lo = _split3(x)
    return _dot(m_bf16, hi) + _dot(m_bf16, mid) + _dot(m_bf16, lo)


def _exact_dot_right(x, m_bf16):
    hi, mid, lo = _split3(x)
    return _dot(hi, m_bf16) + _dot(mid, m_bf16) + _dot(lo, m_bf16)


def _mod_norm(x, shift, scale):
    y = x * lax.rsqrt(jnp.mean(x * x, axis=-1, keepdims=True) + EPS)
    return y * (1.0 + scale) + shift


def _same_group(rows, cols, row_group, col_group):
    r = lax.broadcasted_iota(jnp.int32, (rows, cols), 0) // row_group
    c = lax.broadcasted_iota(jnp.int32, (rows, cols), 1) // col_group
    return r == c


def _mod_kernel(c_ref, w_ref, b_ref, o_ref):
    a = _silu(c_ref[...])
    o_ref[...] = jnp.dot(a, w_ref[...], preferred_element_type=F32,
                         precision=lax.Precision.HIGHEST) + b_ref[...]


def _modulation(cond, w_mod, b_mod):
    depth, d, nd = w_mod.shape
    tn = 1024
    return pl.pallas_call(
        _mod_kernel,
        grid=(depth, nd // tn),
        in_specs=[
            pl.BlockSpec((8, d), lambda l, j: (0, 0)),
            pl.BlockSpec((None, d, tn), lambda l, j: (l, 0, j)),
            pl.BlockSpec((None, 1, tn), lambda l, j: (l, 0, j)),
        ],
        out_specs=pl.BlockSpec((None, 8, tn), lambda l, j: (l, 0, j)),
        out_shape=jax.ShapeDtypeStruct((depth, 8, nd), F32),
        compiler_params=pltpu.CompilerParams(
            dimension_semantics=("arbitrary", "arbitrary"), vmem_limit_bytes=VMEM_LIMIT),
        name="adaln_modulation",
    )(cond, w_mod, b_mod.reshape(depth, 1, nd))


def _ffn_kernel(x_ref, mod_ref, win_ref, wout_ref, o_ref, *, row0, d_ff, chunk):
    x = x_ref[...]
    shift = mod_ref[row0:row0 + 1, :]
    scale = mod_ref[row0 + 1:row0 + 2, :]
    gate = mod_ref[row0 + 2:row0 + 3, :]
    h = _mod_norm(x, shift, scale).astype(BF16)
    acc = None
    for j in range(d_ff // chunk):
        g = _dot(h, win_ref[:, j * chunk:(j + 1) * chunk])
        u = _dot(h, win_ref[:, d_ff + j * chunk:d_ff + (j + 1) * chunk])
        a = (_silu(g) * u).astype(BF16)
        p = _dot(a, wout_ref[j * chunk:(j + 1) * chunk, :])
        acc = p if acc is None else acc + p
    o_ref[...] = x + gate * (0.5 * acc)


def _ffn(xs, mod, w_in, w_out, layer, row0, n_tokens, n_lat, seq, n_batch):
    _, d = xs.shape
    d_ff = w_out.shape[1]
    tm = FFN_TILE
    lat_tiles = n_lat // tm

    def mod_map(i):
        return (layer, jnp.where(i < lat_tiles, (i * tm) // seq, n_batch), 0, 0)

    return pl.pallas_call(
        functools.partial(_ffn_kernel, row0=row0, d_ff=d_ff, chunk=FFN_CHUNK),
        grid=(n_tokens // tm,),
        in_specs=[
            pl.BlockSpec((tm, d), lambda i: (i, 0)),
            pl.BlockSpec((None, None, N_MOD, d), mod_map),
            pl.BlockSpec((None, d, 2 * d_ff), lambda i: (layer, 0, 0)),
            pl.BlockSpec((None, d_ff, d), lambda i: (layer, 0, 0)),
        ],
        out_specs=pl.BlockSpec((tm, d), lambda i: (i, 0)),
        out_shape=jax.ShapeDtypeStruct((n_tokens, d), F32),
        compiler_params=pltpu.CompilerParams(
            dimension_semantics=("arbitrary",), vmem_limit_bytes=VMEM_LIMIT),
        name="swiglu_half_step",
    )(xs, mod, w_in, w_out)


def _head_norm_rope(xh, gain, cos_t, sin_a, sin_b):
    ss = jnp.sum(xh * xh, axis=-1, keepdims=True) * (1.0 / MLA_QK)
    xn = xh * lax.rsqrt(ss + EPS) * gain
    half = MLA_ROPE // 2
    return (xn * cos_t + pltpu.roll(xn, HEAD_PAD - half, 1) * sin_a
            + pltpu.roll(xn, half, 1) * sin_b)


def _inproj_kernel(x_ref, mod_ref, w_ref, lbl_ref, qag_ref, wuq_ref, kvag_ref, wuk_ref, wuv_ref,
                   qg_ref, kg_ref, cos_ref, sina_ref, sinb_ref,
                   hq_ref, hv_ref, lff_ref, kf_ref, lfb_ref, kb_ref, g_ref, pu_ref,
                   q_ref, k_ref, v_ref, *, layer, depth):
    x = x_ref[...]
    h = _mod_norm(x, mod_ref[3:4, :], mod_ref[4:5, :]).astype(BF16)
    y = _dot(h, w_ref[...])

    logits = lbl_ref[...]
    e = jnp.exp(logits - jnp.max(logits, axis=0, keepdims=True))
    p = e / jnp.sum(e, axis=0, keepdims=True)
    lb = jnp.sum(p[:layer + 1], axis=0) - p[0]

    hq_ref[...] = _silu(y[:, C_Q:C_Q + HG_W])
    hv_ref[...] = y[:, C_I:C_I + HG_W]
    g_ref[...] = _silu(y[:, C_G:C_G + HG_W])
    for col, lrow, lf_ref, kk_ref in ((C_FF, 0, lff_ref, kf_ref), (C_FB, 1, lfb_ref, kb_ref)):
        lbv = lb[lrow:lrow + 1, :]
        f = lbv + (1.0 - lbv) * _sigmoid(y[:, col:col + HG_W])
        lf_ref[...] = jnp.log(jnp.clip(f, F_MIN, 1.0))
        kk_ref[...] = 1.0 - f
    pu_ref[...] = y[:, C_POOL:C_POOL + POOL_W]

    cos_t, sin_a, sin_b = cos_ref[...], sina_ref[...], sinb_ref[...]
    cq = y[:, C_CQ:C_CQ + MLA_Q_RANK]
    cqn = cq * lax.rsqrt(jnp.mean(cq * cq, axis=-1, keepdims=True) + EPS) * qag_ref[...]
    qf = _dot(cqn.astype(BF16), wuq_ref[...])
    ckv = y[:, C_CKV:C_CKV + MLA_KV_RANK]
    ckvn = (ckv * lax.rsqrt(jnp.mean(ckv * ckv, axis=-1, keepdims=True) + EPS) * kvag_ref[...]).astype(BF16)
    kn = _dot(ckvn, wuk_ref[...])
    v_ref[...] = _dot(ckvn, wuv_ref[...]).astype(BF16)
    kpe = y[:, C_KPE:C_KPE + HEAD_PAD]
    qk_scale = MLA_QK ** -0.5
    for hd in range(MLA_HEADS):
        sl = slice(hd * HEAD_PAD, (hd + 1) * HEAD_PAD)
        qh = _head_norm_rope(qf[:, sl], qg_ref[...], cos_t, sin_a, sin_b)
        q_ref[:, sl] = (qh * qk_scale).astype(BF16)
        kh = _head_norm_rope(kn[:, sl] + kpe, kg_ref[...], cos_t, sin_a, sin_b)
        k_ref[:, sl] = kh.astype(BF16)


def _inproj(xs, mod, wts, layer, dims):
    n_batch, seq, ctx, n_lat, n_tokens, d, depth = dims
    tm = TOKEN_TILE
    lat_tiles, per_b = n_lat // tm, seq // tm
    s_all = seq + ctx

    def b_of(i):
        return jnp.where(i < lat_tiles, i // per_b, i - lat_tiles)

    def r_of(i):
        return jnp.where(i < lat_tiles, i % per_b, per_b)

    def const(*shape):
        return pl.BlockSpec(shape, lambda i: (0,) * len(shape))

    def per_layer(*shape):
        return pl.BlockSpec((None,) + shape, lambda i: (layer,) + (0,) * len(shape))

    def seq_out(width):
        return pl.BlockSpec((None, tm, width), lambda i: (b_of(i), r_of(i), 0))

    rope_spec = pl.BlockSpec((tm, HEAD_PAD), lambda i: (r_of(i), 0))
    f32_out = jax.ShapeDtypeStruct((n_batch, s_all, HG_W), F32)
    wide = MLA_HEADS * HEAD_PAD
    return pl.pallas_call(
        functools.partial(_inproj_kernel, layer=layer, depth=depth),
        grid=(n_tokens // tm,),
        in_specs=[
            pl.BlockSpec((tm, d), lambda i: (i, 0)),
            pl.BlockSpec((None, None, N_MOD, d), lambda i: (layer, jnp.where(i < lat_tiles, i // per_b, n_batch), 0, 0)),
            per_layer(d, IN_PAD),
            const(depth, 2, HG_W),
            per_layer(1, MLA_Q_RANK), per_layer(MLA_Q_RANK, wide),
            per_layer(1, MLA_KV_RANK), per_layer(MLA_KV_RANK, wide), per_layer(MLA_KV_RANK, MLA_HEADS * MLA_V),
            per_layer(1, HEAD_PAD), per_layer(1, HEAD_PAD),
            rope_spec, rope_spec, rope_spec,
        ],
        out_specs=[seq_out(HG_W)] * 8 + [seq_out(wide), seq_out(wide), seq_out(MLA_HEADS * MLA_V)],
        out_shape=[f32_out] * 8 + [
            jax.ShapeDtypeStruct((n_batch, s_all, wide), BF16),
            jax.ShapeDtypeStruct((n_batch, s_all, wide), BF16),
            jax.ShapeDtypeStruct((n_batch, s_all, MLA_HEADS * MLA_V), BF16)],
        compiler_params=pltpu.CompilerParams(
            dimension_semantics=("arbitrary",), vmem_limit_bytes=VMEM_LIMIT),
        name="mixer_in_projection",
    )(xs, mod, wts["w_in"], wts["lb_logits"], wts["q_a_gain"], wts["w_uq"], wts["kv_a_gain"],
      wts["w_uk"], wts["w_uv"], wts["q_gain"], wts["k_gain"], wts["cos"], wts["sin_a"], wts["sin_b"])


def _hg_chunk(q, k, v, lf, st_ref, rev):
    c, sub = HG_CHUNK, HG_SUB
    n_sub = c // sub
    ri = lax.broadcasted_iota(jnp.int32, (c, c), 0)
    ci = lax.broadcasted_iota(jnp.int32, (c, c), 1)
    tri = ((ci >= ri) if rev else (ci <= ri)).astype(BF16)
    b = _exact_dot_left(tri, lf)
    b_tot = b[0:1, :] if rev else b[c - 1:c, :]
    head_sq = _same_group(HG_W, HG_W, HG_DV, HG_DK)
    head_sub = _same_group(n_sub * sub, HG_W, sub, HG_DK)
    block_ones = head_sq.astype(BF16)

    st = st_ref[...]
    o = _dot_nt((q * jnp.exp(b)).astype(BF16), st.astype(BF16))
    kd = (k * jnp.exp(b_tot - b)).astype(BF16)
    upd = _dot_tn(v.astype(BF16), kd)
    st_ref[...] = st * jnp.exp(b_tot) + jnp.where(head_sq, upd, 0.0)

    def rows(a, i):
        return a[i * sub:(i + 1) * sub, :]

    parts = [[rows(o, i)] for i in range(n_sub)]
    for j in (range(1, n_sub) if rev else range(n_sub - 1)):
        kj, bj, vj = rows(k, j), rows(b, j), rows(v, j)
        ref = bj[0:1, :] if rev else bj[sub - 1:sub, :]
        kh = kj * jnp.exp(ref - bj)
        kh4 = jnp.where(head_sub, jnp.concatenate([kh] * HG_HEADS, axis=0), 0.0).astype(BF16)
        v4 = jnp.where(head_sub, jnp.concatenate([vj] * HG_HEADS, axis=0), 0.0).astype(BF16)
        lo, hi = (0, j * sub) if rev else ((j + 1) * sub, c)
        qt = (q[lo:hi, :] * jnp.exp(b[lo:hi, :] - ref)).astype(BF16)
        att = _dot_nt(qt, kh4)
        oa = _dot(att.astype(BF16), v4)
        for i in range(lo // sub, hi // sub):
            parts[i].append(oa[i * sub - lo:(i + 1) * sub - lo, :])
    cidx = lax.broadcasted_iota(jnp.int32, (sub, HG_W), 0)
    pieces = []
    for i in range(n_sub):
        qi, ki, bi = rows(q, i), rows(k, i), rows(b, i)
        for s in range(sub):
            dec = jnp.exp(jnp.minimum(bi - bi[s:s + 1, :], 0.0))
            e = qi * ki[s:s + 1, :] * dec
            keep = (cidx <= s) if rev else (cidx >= s)
            pieces.append(jnp.where(keep, e, 0.0).astype(BF16))
    r = _dot(jnp.concatenate(pieces, axis=0), block_ones)
    for i in range(n_sub):
        vi = rows(v, i)
        for s in range(sub):
            off = (i * sub + s) * sub
            parts[i].append(r[off:off + sub, :] * vi[s:s + 1, :])
    outs = []
    for plist in parts:
        tot = plist[0]
        for t in plist[1:]:
            tot = tot + t
        outs.append(tot)
    return jnp.concatenate(outs, axis=0)


def _hgrn_kernel(qf_ref, vf_ref, lff_ref, kf_ref, qb_ref, vb_ref, lfb_ref, kb_ref,
                 of_ref, ob_ref, sf_ref, sb_ref, *, tb):
    @pl.when(pl.program_id(1) == 0)
    def _():
        sf_ref[...] = jnp.zeros_like(sf_ref)
        sb_ref[...] = jnp.zeros_like(sb_ref)

    n_chunks = tb // HG_CHUNK

    def body(ci, carry):
        sl = pl.ds(pl.multiple_of(ci * HG_CHUNK, HG_CHUNK), HG_CHUNK)
        of_ref[sl, :] = _hg_chunk(qf_ref[sl, :], kf_ref[sl, :], vf_ref[sl, :], lff_ref[sl, :], sf_ref, False)
        sr = pl.ds(pl.multiple_of((n_chunks - 1 - ci) * HG_CHUNK, HG_CHUNK), HG_CHUNK)
        ob_ref[sr, :] = _hg_chunk(qb_ref[sr, :], kb_ref[sr, :], vb_ref[sr, :], lfb_ref[sr, :], sb_ref, True)
        return carry

    lax.fori_loop(0, n_chunks, body, 0)


def _hgrn(hq, hv, lff, kf, lfb, kb, dims):
    n_batch, seq, ctx = dims[0], dims[1], dims[2]
    tb = TOKEN_TILE
    assert ctx == tb and seq % tb == 0
    per_b = seq // tb
    fwd = pl.BlockSpec((None, tb, HG_W), lambda b, n: (b, jnp.where(n == 0, per_b, n - 1), 0))
    bwd = pl.BlockSpec((None, tb, HG_W), lambda b, n: (b, jnp.where(n == 0, per_b, per_b - n), 0))
    shape = jax.ShapeDtypeStruct(hq.shape, F32)
    return pl.pallas_call(
        functools.partial(_hgrn_kernel, tb=tb),
        grid=(n_batch, per_b + 1),
        in_specs=[fwd, fwd, fwd, fwd, bwd, bwd, bwd, bwd],
        out_specs=[fwd, bwd],
        out_shape=[shape, shape],
        scratch_shapes=[pltpu.VMEM((HG_W, HG_W), F32), pltpu.VMEM((HG_W, HG_W), F32)],
        compiler_params=pltpu.CompilerParams(
            dimension_semantics=("arbitrary", "arbitrary"), vmem_limit_bytes=VMEM_LIMIT),
        name="hgrn2_bidirectional_scan",
    )(hq, hv, lff, kf, hq, hv, lfb, kb)


def _attn_kernel(q_ref, k_ref, v_ref, o_ref, *, tk, n_kblocks, lat_kblocks):
    tq = q_ref.shape[0]
    start = jnp.where(pl.program_id(2) == pl.num_programs(2) - 1, lat_kblocks, 0)
    outs = []
    for hh in range(2):
        q = q_ref[:, hh * HEAD_PAD:(hh + 1) * HEAD_PAD]

        def body(j, carry, hh=hh, q=q):
            m, l, acc = carry
            ks = pl.ds(pl.multiple_of(j * tk, tk), tk)
            s = _dot_nt(q, k_ref[ks, hh * HEAD_PAD:(hh + 1) * HEAD_PAD])
            m_new = jnp.maximum(m, jnp.max(s, axis=-1, keepdims=True))
            alpha = jnp.exp(m - m_new)
            p = jnp.exp(s - m_new)
            l = alpha * l + jnp.sum(p, axis=-1, keepdims=True)
            acc = alpha * acc + _dot(p.astype(BF16), v_ref[ks, :])
            return m_new, l, acc

        init = (jnp.full((tq, 1), -1e30, F32), jnp.zeros((tq, 1), F32), jnp.zeros((tq, 2 * MLA_V), F32))
        m, l, acc = lax.fori_loop(start, n_kblocks, body, init)
        outs.append(acc / l)
    lane = lax.broadcasted_iota(jnp.int32, (tq, 2 * MLA_V), 1)
    o_ref[...] = jnp.where(lane < MLA_V, outs[0], outs[1]).astype(o_ref.dtype)


def _attention(q, k, v, dims):
    n_batch, seq, ctx = dims[0], dims[1], dims[2]
    s_all = seq + ctx
    tq, tk = TOKEN_TILE, ATTN_KEY_TILE
    assert ctx == tq and seq % tk == 0 and ctx % tk == 0
    pairs = MLA_HEADS // 2
    return pl.pallas_call(
        functools.partial(_attn_kernel, tk=tk, n_kblocks=s_all // tk, lat_kblocks=seq // tk),
        grid=(n_batch, pairs, s_all // tq),
        in_specs=[
            pl.BlockSpec((None, tq, 2 * HEAD_PAD), lambda b, p, i: (b, i, p)),
            pl.BlockSpec((None, s_all, 2 * HEAD_PAD), lambda b, p, i: (b, 0, p)),
            pl.BlockSpec((None, s_all, 2 * MLA_V), lambda b, p, i: (b, 0, p)),
        ],
        out_specs=pl.BlockSpec((None, tq, 2 * MLA_V), lambda b, p, i: (b, i, p)),
        out_shape=jax.ShapeDtypeStruct((n_batch, s_all, MLA_HEADS * MLA_V), BF16),
        compiler_params=pltpu.CompilerParams(
            dimension_semantics=("arbitrary", "arbitrary", "arbitrary"), vmem_limit_bytes=VMEM_LIMIT),
        name="latent_attention",
    )(q, k, v)


def _outproj_kernel(x_ref, mod_ref, of_ref, ob_ref, g_ref, att_ref, pu_ref, pprev_ref, pnext_ref,
                    hgain_ref, wp_ref, pscale_ref, wout_ref, o_ref, *, lat_tiles, per_b, seq, ctx):
    i = pl.program_id(0)
    tm = x_ref.shape[0]
    is_lat = i < lat_tiles
    r = jnp.where(is_lat, i % per_b, 0)
    seq_len = jnp.where(is_lat, seq, ctx)
    seq_tiles = jnp.where(is_lat, per_b, ctx // tm)

    o = of_ref[...] + ob_ref[...]
    block_ones = _same_group(HG_W, HG_W, HG_DV, HG_DV).astype(BF16)
    ms = _exact_dot_right(o * o, block_ones) * (1.0 / HG_DV)
    hg = o * lax.rsqrt(ms + EPS) * hgain_ref[...] * g_ref[...]

    h8 = POOL_HALO
    prev = jnp.where(r > 0, pprev_ref[...], 0.0)
    nxt = jnp.where(r < seq_tiles - 1, pnext_ref[...], 0.0)
    u = pu_ref[...]
    ext = jnp.concatenate([prev, u, nxt], axis=0)
    n_ext = tm + 2 * h8

    def shifted(a, delta):
        return pltpu.roll(a, (-delta) % n_ext, 0)

    s2 = ext + shifted(ext, -1)
    s4 = shifted(s2, -1) + shifted(s2, 1)
    s8 = shifted(s4, -2) + shifted(s4, 2)
    s16 = shifted(s8, -4) + shifted(s8, 4)
    pos = r * tm + lax.broadcasted_iota(jnp.int32, (tm, POOL_W), 0)
    lane_g = lax.broadcasted_iota(jnp.int32, (tm, POOL_W), 1) // POOL_CH
    pooled = jnp.zeros((tm, POOL_W), F32)
    for g, (w, sums) in enumerate(zip(POOL_WINDOWS, (s2, s4, s8, s16))):
        lo = jnp.maximum(pos - w // 2, 0)
        hi = jnp.minimum(pos + w - 1 - w // 2, seq_len - 1)
        cnt = (hi - lo + 1).astype(F32)
        pooled = jnp.where(lane_g == g, sums[h8:h8 + tm, :] / cnt - u, pooled)
    pool = _dot(pooled.astype(BF16), wp_ref[...]) * pscale_ref[...]

    mix = jnp.concatenate([hg.astype(BF16), att_ref[...], pool.astype(BF16)], axis=1)
    o_ref[...] = x_ref[...] + mod_ref[5:6, :] * _dot(mix, wout_ref[...])


def _outproj(xs, mod, o_f, o_b, g, att, pu, wts, layer, dims, with_ctx):
    n_batch, seq, ctx, n_lat, n_tokens, d, depth = dims
    tm = TOKEN_TILE
    lat_tiles, per_b = n_lat // tm, seq // tm
    n_out = n_tokens if with_ctx else n_lat
    s_all = seq + ctx
    hb = tm // POOL_HALO

    def b_of(i):
        return jnp.where(i < lat_tiles, i // per_b, i - lat_tiles)

    def r_of(i):
        return jnp.where(i < lat_tiles, i % per_b, per_b)

    def seq_in(width):
        return pl.BlockSpec((None, tm, width), lambda i: (b_of(i), r_of(i), 0))

    def per_layer(*shape):
        return pl.BlockSpec((None,) + shape, lambda i: (layer,) + (0,) * len(shape))

    last_halo = s_all // POOL_HALO - 1
    prev_spec = pl.BlockSpec((None, POOL_HALO, POOL_W),
                             lambda i: (b_of(i), jnp.maximum(r_of(i) * hb - 1, 0), 0))
    next_spec = pl.BlockSpec((None, POOL_HALO, POOL_W),
                             lambda i: (b_of(i), jnp.minimum((r_of(i) + 1) * hb, last_halo), 0))
    return pl.pallas_call(
        functools.partial(_outproj_kernel, lat_tiles=lat_tiles, per_b=per_b, seq=seq, ctx=ctx),
        grid=(n_out // tm,),
        in_specs=[
            pl.BlockSpec((tm, d), lambda i: (i, 0)),
            pl.BlockSpec((None, None, N_MOD, d), lambda i: (layer, jnp.where(i < lat_tiles, i // per_b, n_batch), 0, 0)),
            seq_in(HG_W), seq_in(HG_W), seq_in(HG_W), seq_in(MLA_HEADS * MLA_V), seq_in(POOL_W),
            prev_spec, next_spec,
            per_layer(1, HG_W), per_layer(POOL_W, POOL_W), per_layer(1, POOL_W), per_layer(d, d),
        ],
        out_specs=pl.BlockSpec((tm, d), lambda i: (i, 0)),
        out_shape=jax.ShapeDtypeStruct((n_out, d), F32),
        compiler_params=pltpu.CompilerParams(
            dimension_semantics=("arbitrary",), vmem_limit_bytes=VMEM_LIMIT),
        name="mixer_out_projection",
    )(xs, mod, o_f, o_b, g, att, pu, pu, pu, wts["hg_gain"], wts["w_pool"], wts["pool_scale"], wts["w_out"])


def _rope_tables(seq, ctx):
    rows = seq // GRID_W
    row = np.repeat(np.arange(rows), GRID_W).astype(np.float32)
    col = np.tile(np.arange(GRID_W), rows).astype(np.float32)
    n_freq = MLA_ROPE // 4
    inv_freq = jnp.asarray(ROPE_BASE, F32) ** (-jnp.arange(n_freq, dtype=F32) / n_freq)
    ang = jnp.concatenate([jnp.asarray(row)[:, None] * inv_freq, jnp.asarray(col)[:, None] * inv_freq], axis=-1)
    cos, sin = jnp.cos(ang), jnp.sin(ang)
    half = MLA_ROPE // 2
    zeros = lambda w: jnp.zeros((seq, w), F32)
    cos_t = jnp.concatenate([jnp.ones((seq, MLA_NOPE), F32), cos, cos, zeros(HEAD_PAD - MLA_QK)], axis=1)
    sin_a = jnp.concatenate([zeros(MLA_NOPE), -sin, zeros(HEAD_PAD - MLA_NOPE - half)], axis=1)
    sin_b = jnp.concatenate([zeros(MLA_NOPE + half), sin, zeros(HEAD_PAD - MLA_QK)], axis=1)
    ident = jnp.concatenate([jnp.ones((ctx, MLA_QK), F32), jnp.zeros((ctx, HEAD_PAD - MLA_QK), F32)], axis=1)
    pad0 = jnp.zeros((ctx, HEAD_PAD), F32)
    return (jnp.concatenate([cos_t, ident], axis=0), jnp.concatenate([sin_a, pad0], axis=0),
            jnp.concatenate([sin_b, pad0], axis=0))


def _prepare_weights(p, dims):
    seq, ctx_len, d, depth = dims[1], dims[2], dims[5], dims[6]
    pad_cols = lambda a, n: jnp.pad(a, [(0, 0)] * (a.ndim - 1) + [(0, n)])
    w_in = p["w_in"]
    w_in_p = jnp.concatenate([
        w_in[:, :, :C_POOL], w_in[:, :, C_POOL + MLA_ROPE:],
        jnp.zeros((depth, d, MLA_NOPE), F32), w_in[:, :, C_POOL:C_POOL + MLA_ROPE],
        jnp.zeros((depth, d, HEAD_PAD - MLA_QK), F32)], axis=-1).astype(BF16)
    w_uq = pad_cols(p["mla_w_uq"].reshape(depth, MLA_Q_RANK, MLA_HEADS, MLA_QK), HEAD_PAD - MLA_QK)
    w_ukv = p["mla_w_ukv"].reshape(depth, MLA_KV_RANK, MLA_HEADS, MLA_NOPE + MLA_V)
    cos_t, sin_a, sin_b = _rope_tables(seq, ctx_len)
    eye = jnp.eye(len(POOL_WINDOWS), dtype=F32)
    wts = {
        "w_in": w_in_p,
        "lb_logits": p["hg_lb_logits"].astype(F32),
        "q_a_gain": p["mla_q_a_gain"].reshape(depth, 1, MLA_Q_RANK),
        "w_uq": w_uq.reshape(depth, MLA_Q_RANK, MLA_HEADS * HEAD_PAD).astype(BF16),
        "kv_a_gain": p["mla_kv_a_gain"].reshape(depth, 1, MLA_KV_RANK),
        "w_uk": pad_cols(w_ukv[..., :MLA_NOPE], HEAD_PAD - MLA_NOPE).reshape(
            depth, MLA_KV_RANK, MLA_HEADS * HEAD_PAD).astype(BF16),
        "w_uv": w_ukv[..., MLA_NOPE:].reshape(depth, MLA_KV_RANK, MLA_HEADS * MLA_V).astype(BF16),
        "q_gain": pad_cols(p["mla_q_gain"], HEAD_PAD - MLA_QK).reshape(depth, 1, HEAD_PAD),
        "k_gain": pad_cols(p["mla_k_gain"], HEAD_PAD - MLA_QK).reshape(depth, 1, HEAD_PAD),
        "cos": cos_t, "sin_a": sin_a, "sin_b": sin_b,
        "hg_gain": jnp.tile(p["hg_out_gain"], (1, HG_HEADS)).reshape(depth, 1, HG_W),
        "w_pool": jnp.einsum("gh,lgcd->lgchd", eye, p["pool_w"]).reshape(depth, POOL_W, POOL_W).astype(BF16),
        "pool_scale": p["pool_scale"].reshape(depth, 1, POOL_W),
        "w_out": p["w_out"].astype(BF16),
    }
    ffn = tuple(p[n].astype(BF16) for n in ("ffn1_w_in", "ffn1_w_out", "ffn2_w_in", "ffn2_w_out"))
    return wts, ffn


def kernel(x, c, ctx, c_ctx, w_mod, b_mod, ffn1_w_in, ffn1_w_out, w_in, w_out, hg_lb_logits,
           hg_out_gain, mla_q_a_gain, mla_w_uq, mla_kv_a_gain, mla_w_ukv, mla_q_gain, mla_k_gain,
           pool_w, pool_scale, ffn2_w_in, ffn2_w_out):
    n_batch, seq, d = x.shape
    ctx_len = ctx.shape[1]
    depth = w_mod.shape[0]
    n_lat, n_ctx = n_batch * seq, n_batch * ctx_len
    n_tokens = n_lat + n_ctx
    dims = (n_batch, seq, ctx_len, n_lat, n_tokens, d, depth)
    assert n_batch + 1 <= 8 and seq % FFN_TILE == 0 and n_ctx % FFN_TILE == 0
    wts, (f1_in, f1_out, f2_in, f2_out) = _prepare_weights(dict(
        w_in=w_in, w_out=w_out, hg_lb_logits=hg_lb_logits, hg_out_gain=hg_out_gain,
        mla_q_a_gain=mla_q_a_gain, mla_w_uq=mla_w_uq, mla_kv_a_gain=mla_kv_a_gain, mla_w_ukv=mla_w_ukv,
        mla_q_gain=mla_q_gain, mla_k_gain=mla_k_gain, pool_w=pool_w, pool_scale=pool_scale,
        ffn1_w_in=ffn1_w_in, ffn1_w_out=ffn1_w_out, ffn2_w_in=ffn2_w_in, ffn2_w_out=ffn2_w_out), dims)

    cond = jnp.concatenate([c, c_ctx[None, :], jnp.zeros((8 - n_batch - 1, d), F32)], axis=0)
    mod = _modulation(cond, w_mod, b_mod).reshape(depth, 8, N_MOD, d)

    xs = jnp.concatenate([x.reshape(n_lat, d), ctx.reshape(n_ctx, d)], axis=0)
    for l in range(depth):
        last = l == depth - 1
        xs = _ffn(xs, mod, f1_in, f1_out, l, 0, n_tokens, n_lat, seq, n_batch)
        hq, hv, lff, kf, lfb, kb, g, pu, q, k, v = _inproj(xs, mod, wts, l, dims)
        o_f, o_b = _hgrn(hq, hv, lff, kf, lfb, kb, dims)
        att = _attention(q, k, v, dims)
        xs = _outproj(xs, mod, o_f, o_b, g, att, pu, wts, l, dims, with_ctx=not last)
        xs = _ffn(xs, mod, f2_in, f2_out, l, 6, n_lat if last else n_tokens, n_lat, seq, n_batch)
    return xs.reshape(n_batch, seq, d)
```

```python
import functools

import numpy as np
import jax
import jax.numpy as jnp
from jax import lax
from jax.experimental import pallas as pl
from jax.experimental.pallas import tpu as pltpu

F32 = jnp.float32
BF16 = jnp.bfloat16
EPS = 1e-6
F_MIN = 1e-6
N_MOD = 9
GRID_W = 64
ROPE_BASE = 10000.0
HG_HEADS = 4
HG_DK = 64
HG_DV = 64
HG_W = HG_HEADS * HG_DK
HG_CHUNK = 64
HG_SUB = 16
MLA_HEADS = 8
MLA_Q_RANK = 384
MLA_KV_RANK = 256
MLA_NOPE = 64
MLA_ROPE = 32
MLA_V = 64
MLA_QK = MLA_NOPE + MLA_ROPE
HEAD_PAD = 128
POOL_WINDOWS = (2, 4, 8, 16)
POOL_CH = 64
POOL_W = len(POOL_WINDOWS) * POOL_CH
POOL_HALO = 8
TOKEN_TILE = 256
FFN_TILE = 512
FFN_CHUNK = 256
LOG2_E = 1.4426950408889634
VMEM_LIMIT = 56 * 1024 * 1024

C_Q, C_FF, C_FB, C_I, C_G = 0, 256, 512, 768, 1024
C_CQ = 1280
C_CKV = C_CQ + MLA_Q_RANK
C_POOL = C_CKV + MLA_KV_RANK
C_KPE = C_POOL + POOL_W
IN_PAD = C_KPE + HEAD_PAD


def _dot(a, b):
    return jnp.dot(a, b, preferred_element_type=F32)


def _dot_nt(a, b):
    return lax.dot_general(a, b, (((1,), (1,)), ((), ())), preferred_element_type=F32)


def _dot_tn(a, b):
    return lax.dot_general(a, b, (((0,), (0,)), ((), ())), preferred_element_type=F32)


def _sigmoid(x):
    return 1.0 / (1.0 + jnp.exp(-x))


def _silu(x):
    return x * _sigmoid(x)


def _split3(x):
    hi = x.astype(BF16)
    r = x - hi.astype(F32)
    mid = r.astype(BF16)
    lo = (r - mid.astype(F32)).astype(BF16)
    return hi, mid, lo


def _exact_dot_left(m_bf16, x):
    hi, mid, lo = _split3(x)
    return _dot(m_bf16, hi) + _dot(m_bf16, mid) + _dot(m_bf16, lo)


def _exact_dot_right(x, m_bf16):
    hi, mid, lo = _split3(x)
    return _dot(hi, m_bf16) + _dot(mid, m_bf16) + _dot(lo, m_bf16)


def _mod_norm(x, shift, scale):
    y = x * lax.rsqrt(jnp.mean(x * x, axis=-1, keepdims=True) + EPS)
    return y * (1.0 + scale) + shift


def _same_group(rows, cols, row_group, col_group):
    r = lax.broadcasted_iota(jnp.int32, (rows, cols), 0) // row_group
    c = lax.broadcasted_iota(jnp.int32, (rows, cols), 1) // col_group
    return r == c


def _mod_kernel(c_ref, w_ref, b_ref, o_ref):
    a = _silu(c_ref[...])
    o_ref[...] = jnp.dot(a, w_ref[...], preferred_element_type=F32,
                         precision=lax.Precision.HIGHEST) + b_ref[...]


def _modulation(cond, w_mod, b_mod):
    depth, d, nd = w_mod.shape
    tn = 1024
    return pl.pallas_call(
        _mod_kernel,
        grid=(depth, nd // tn),
        in_specs=[
            pl.BlockSpec((8, d), lambda l, j: (0, 0)),
            pl.BlockSpec((None, d, tn), lambda l, j: (l, 0, j)),
            pl.BlockSpec((None, 1, tn), lambda l, j: (l, 0, j)),
        ],
        out_specs=pl.BlockSpec((None, 8, tn), lambda l, j: (l, 0, j)),
        out_shape=jax.ShapeDtypeStruct((depth, 8, nd), F32),
        compiler_params=pltpu.CompilerParams(
            dimension_semantics=("arbitrary", "arbitrary"), vmem_limit_bytes=VMEM_LIMIT),
        name="adaln_modulation",
    )(cond, w_mod, b_mod.reshape(depth, 1, nd))


def _ffn_kernel(x_ref, mod_ref, win_ref, wout_ref, o_ref, *, row0, d_ff, chunk):
    x = x_ref[...]
    shift = mod_ref[row0:row0 + 1, :]
    scale = mod_ref[row0 + 1:row0 + 2, :]
    gate = mod_ref[row0 + 2:row0 + 3, :]
    h = _mod_norm(x, shift, scale).astype(BF16)
    acc = None
    for j in range(d_ff // chunk):
        g = _dot(h, win_ref[:, j * chunk:(j + 1) * chunk])
        u = _dot(h, win_ref[:, d_ff + j * chunk:d_ff + (j + 1) * chunk])
        a = (_silu(g) * u).astype(BF16)
        p = _dot(a, wout_ref[j * chunk:(j + 1) * chunk, :])
        acc = p if acc is None else acc + p
    o_ref[...] = x + gate * (0.5 * acc)


def _ffn(xs, mod, w_in, w_out, layer, row0, n_tokens, n_lat, seq, n_batch):
    _, d = xs.shape
    d_ff = w_out.shape[1]
    tm = FFN_TILE
    lat_tiles = n_lat // tm

    def mod_map(i):
        return (layer, jnp.where(i < lat_tiles, (i * tm) // seq, n_batch), 0, 0)

    return pl.pallas_call(
        functools.partial(_ffn_kernel, row0=row0, d_ff=d_ff, chunk=FFN_CHUNK),
        grid=(n_tokens // tm,),
        in_specs=[
            pl.BlockSpec((tm, d), lambda i: (i, 0)),
            pl.BlockSpec((None, None, N_MOD, d), mod_map),
            pl.BlockSpec((None, d, 2 * d_ff), lambda i: (layer, 0, 0)),
            pl.BlockSpec((None, d_ff, d), lambda i: (layer, 0, 0)),
        ],
        out_specs=pl.BlockSpec((tm, d), lambda i: (i, 0)),
        out_shape=jax.ShapeDtypeStruct((n_tokens, d), F32),
        compiler_params=pltpu.CompilerParams(
            dimension_semantics=("arbitrary",), vmem_limit_bytes=VMEM_LIMIT),
        name="swiglu_half_step",
    )(xs, mod, w_in, w_out)


def _head_norm_rope(xh, gain, cos_t, sin_a, sin_b):
    ss = jnp.sum(xh * xh, axis=-1, keepdims=True) * (1.0 / MLA_QK)
    xn = xh * lax.rsqrt(ss + EPS) * gain
    half = MLA_ROPE // 2
    return (xn * cos_t + pltpu.roll(xn, HEAD_PAD - half, 1) * sin_a
            + pltpu.roll(xn, half, 1) * sin_b)


def _inproj_kernel(x_ref, mod_ref, w_ref, lbl_ref, qag_ref, wuq_ref, kvag_ref, wuk_ref, wuv_ref,
                   qg_ref, kg_ref, cos_ref, sina_ref, sinb_ref,
                   hq_ref, hv_ref, lff_ref, kf_ref, lfb_ref, kb_ref, g_ref, pu_ref,
                   q_ref, k_ref, v_ref, *, layer, depth):
    x = x_ref[...]
    h = _mod_norm(x, mod_ref[3:4, :], mod_ref[4:5, :]).astype(BF16)
    y = _dot(h, w_ref[...])

    logits = lbl_ref[...]
    e = jnp.exp(logits - jnp.max(logits, axis=0, keepdims=True))
    p = e / jnp.sum(e, axis=0, keepdims=True)
    lb = jnp.sum(p[:layer + 1], axis=0) - p[0]

    hq_ref[...] = _silu(y[:, C_Q:C_Q + HG_W])
    hv_ref[...] = y[:, C_I:C_I + HG_W]
    g_ref[...] = _silu(y[:, C_G:C_G + HG_W])
    for col, lrow, lf_ref, kk_ref in ((C_FF, 0, lff_ref, kf_ref), (C_FB, 1, lfb_ref, kb_ref)):
        lbv = lb[lrow:lrow + 1, :]
        f = lbv + (1.0 - lbv) * _sigmoid(y[:, col:col + HG_W])
        lf_ref[...] = jnp.log(jnp.clip(f, F_MIN, 1.0))
        kk_ref[...] = 1.0 - f
    pu_ref[...] = y[:, C_POOL:C_POOL + POOL_W]

    cos_t, sin_a, sin_b = cos_ref[...], sina_ref[...], sinb_ref[...]
    cq = y[:, C_CQ:C_CQ + MLA_Q_RANK]
    cqn = cq * lax.rsqrt(jnp.mean(cq * cq, axis=-1, keepdims=True) + EPS) * qag_ref[...]
    qf = _dot(cqn.astype(BF16), wuq_ref[...])
    ckv = y[:, C_CKV:C_CKV + MLA_KV_RANK]
    ckvn = (ckv * lax.rsqrt(jnp.mean(ckv * ckv, axis=-1, keepdims=True) + EPS) * kvag_ref[...]).astype(BF16)
    kn = _dot(ckvn, wuk_ref[...])
    v_ref[...] = _dot(ckvn, wuv_ref[...]).astype(BF16)
    kpe = y[:, C_KPE:C_KPE + HEAD_PAD]
    qk_scale = MLA_QK ** -0.5 * LOG2_E
    for hd in range(MLA_HEADS):
        sl = slice(hd * HEAD_PAD, (hd + 1) * HEAD_PAD)
        qh = _head_norm_rope(qf[:, sl], qg_ref[...], cos_t, sin_a, sin_b)
        q_ref[:, sl] = (qh * qk_scale).astype(BF16)
        kh = _head_norm_rope(kn[:, sl] + kpe, kg_ref[...], cos_t, sin_a, sin_b)
        k_ref[sl, :] = kh.T.astype(BF16)


def _inproj(xs, mod, wts, layer, dims):
    n_batch, seq, ctx, n_lat, n_tokens, d, depth = dims
    tm = TOKEN_TILE
    lat_tiles, per_b = n_lat // tm, seq // tm
    s_all = seq + ctx

    def b_of(i):
        return jnp.where(i < lat_tiles, i // per_b, i - lat_tiles)

    def r_of(i):
        return jnp.where(i < lat_tiles, i % per_b, per_b)

    def const(*shape):
        return pl.BlockSpec(shape, lambda i: (0,) * len(shape))

    def per_layer(*shape):
        return pl.BlockSpec((None,) + shape, lambda i: (layer,) + (0,) * len(shape))

    def seq_out(width):
        return pl.BlockSpec((None, tm, width), lambda i: (b_of(i), r_of(i), 0))

    rope_spec = pl.BlockSpec((tm, HEAD_PAD), lambda i: (r_of(i), 0))
    f32_out = jax.ShapeDtypeStruct((n_batch, s_all, HG_W), F32)
    wide = MLA_HEADS * HEAD_PAD
    return pl.pallas_call(
        functools.partial(_inproj_kernel, layer=layer, depth=depth),
        grid=(n_tokens // tm,),
        in_specs=[
            pl.BlockSpec((tm, d), lambda i: (i, 0)),
            pl.BlockSpec((None, None, N_MOD, d), lambda i: (layer, jnp.where(i < lat_tiles, i // per_b, n_batch), 0, 0)),
            per_layer(d, IN_PAD),
            const(depth, 2, HG_W),
            per_layer(1, MLA_Q_RANK), per_layer(MLA_Q_RANK, wide),
            per_layer(1, MLA_KV_RANK), per_layer(MLA_KV_RANK, wide), per_layer(MLA_KV_RANK, MLA_HEADS * MLA_V),
            per_layer(1, HEAD_PAD), per_layer(1, HEAD_PAD),
            rope_spec, rope_spec, rope_spec,
        ],
        out_specs=[seq_out(HG_W)] * 8 + [
            seq_out(wide), pl.BlockSpec((None, wide, tm), lambda i: (b_of(i), 0, r_of(i))),
            seq_out(MLA_HEADS * MLA_V)],
        out_shape=[f32_out] * 8 + [
            jax.ShapeDtypeStruct((n_batch, s_all, wide), BF16),
            jax.ShapeDtypeStruct((n_batch, wide, s_all), BF16),
            jax.ShapeDtypeStruct((n_batch, s_all, MLA_HEADS * MLA_V), BF16)],
        compiler_params=pltpu.CompilerParams(
            dimension_semantics=("arbitrary",), vmem_limit_bytes=VMEM_LIMIT),
        name="mixer_in_projection",
    )(xs, mod, wts["w_in"], wts["lb_logits"], wts["q_a_gain"], wts["w_uq"], wts["kv_a_gain"],
      wts["w_uk"], wts["w_uv"], wts["q_gain"], wts["k_gain"], wts["cos"], wts["sin_a"], wts["sin_b"])


def _hg_chunk(q, k, v, lf, st_ref, rev):
    c, sub = HG_CHUNK, HG_SUB
    n_sub = c // sub
    ri = lax.broadcasted_iota(jnp.int32, (c, c), 0)
    ci = lax.broadcasted_iota(jnp.int32, (c, c), 1)
    tri = ((ci >= ri) if rev else (ci <= ri)).astype(BF16)
    b = _exact_dot_left(tri, lf)
    b_tot = b[0:1, :] if rev else b[c - 1:c, :]
    head_sq = _same_group(HG_W, HG_W, HG_DV, HG_DK)
    head_sub = _same_group(n_sub * sub, HG_W, sub, HG_DK)
    block_ones = head_sq.astype(BF16)

    st = st_ref[...]
    o = _dot_nt((q * jnp.exp(b)).astype(BF16), st.astype(BF16))
    kd = (k * jnp.exp(b_tot - b)).astype(BF16)
    upd = _dot_tn(v.astype(BF16), kd)
    st_ref[...] = st * jnp.exp(b_tot) + jnp.where(head_sq, upd, 0.0)

    def rows(a, i):
        return a[i * sub:(i + 1) * sub, :]

    parts = [[rows(o, i)] for i in range(n_sub)]
    for j in (range(1, n_sub) if rev else range(n_sub - 1)):
        kj, bj, vj = rows(k, j), rows(b, j), rows(v, j)
        ref = bj[0:1, :] if rev else bj[sub - 1:sub, :]
        kh = kj * jnp.exp(ref - bj)
        kh4 = jnp.where(head_sub, jnp.concatenate([kh] * HG_HEADS, axis=0), 0.0).astype(BF16)
        v4 = jnp.where(head_sub, jnp.concatenate([vj] * HG_HEADS, axis=0), 0.0).astype(BF16)
        lo, hi = (0, j * sub) if rev else ((j + 1) * sub, c)
        qt = (q[lo:hi, :] * jnp.exp(b[lo:hi, :] - ref)).astype(BF16)
        att = _dot_nt(qt, kh4)
        oa = _dot(att.astype(BF16), v4)
        for i in range(lo // sub, hi // sub):
            parts[i].append(oa[i * sub - lo:(i + 1) * sub - lo, :])
    cidx = lax.broadcasted_iota(jnp.int32, (sub, HG_W), 0)
    pieces = []
    for i in range(n_sub):
        qi, ki, bi = rows(q, i), rows(k, i), rows(b, i)
        for s in range(sub):
            dec = jnp.exp(jnp.minimum(bi - bi[s:s + 1, :], 0.0))
            e = qi * ki[s:s + 1, :] * dec
            keep = (cidx <= s) if rev else (cidx >= s)
            pieces.append(jnp.where(keep, e, 0.0).astype(BF16))
    r = _dot(jnp.concatenate(pieces, axis=0), block_ones)
    for i in range(n_sub):
        vi = rows(v, i)
        for s in range(sub):
            off = (i * sub + s) * sub
            parts[i].append(r[off:off + sub, :] * vi[s:s + 1, :])
    outs = []
    for plist in parts:
        tot = plist[0]
        for t in plist[1:]:
            tot = tot + t
        outs.append(tot)
    return jnp.concatenate(outs, axis=0)


def _hgrn_kernel(qf_ref, vf_ref, lff_ref, kf_ref, qb_ref, vb_ref, lfb_ref, kb_ref,
                 of_ref, ob_ref, sf_ref, sb_ref, *, tb):
    @pl.when(pl.program_id(1) == 0)
    def _():
        sf_ref[...] = jnp.zeros_like(sf_ref)
        sb_ref[...] = jnp.zeros_like(sb_ref)

    n_chunks = tb // HG_CHUNK

    def body(ci, carry):
        sl = pl.ds(pl.multiple_of(ci * HG_CHUNK, HG_CHUNK), HG_CHUNK)
        of_ref[sl, :] = _hg_chunk(qf_ref[sl, :], kf_ref[sl, :], vf_ref[sl, :], lff_ref[sl, :], sf_ref, False)
        sr = pl.ds(pl.multiple_of((n_chunks - 1 - ci) * HG_CHUNK, HG_CHUNK), HG_CHUNK)
        ob_ref[sr, :] = _hg_chunk(qb_ref[sr, :], kb_ref[sr, :], vb_ref[sr, :], lfb_ref[sr, :], sb_ref, True)
        return carry

    lax.fori_loop(0, n_chunks, body, 0)


def _hgrn(hq, hv, lff, kf, lfb, kb, dims):
    n_batch, seq, ctx = dims[0], dims[1], dims[2]
    tb = TOKEN_TILE
    assert ctx == tb and seq % tb == 0
    per_b = seq // tb
    fwd = pl.BlockSpec((None, tb, HG_W), lambda b, n: (b, jnp.where(n == 0, per_b, n - 1), 0))
    bwd = pl.BlockSpec((None, tb, HG_W), lambda b, n: (b, jnp.where(n == 0, per_b, per_b - n), 0))
    shape = jax.ShapeDtypeStruct(hq.shape, F32)
    return pl.pallas_call(
        functools.partial(_hgrn_kernel, tb=tb),
        grid=(n_batch, per_b + 1),
        in_specs=[fwd, fwd, fwd, fwd, bwd, bwd, bwd, bwd],
        out_specs=[fwd, bwd],
        out_shape=[shape, shape],
        scratch_shapes=[pltpu.VMEM((HG_W, HG_W), F32), pltpu.VMEM((HG_W, HG_W), F32)],
        compiler_params=pltpu.CompilerParams(
            dimension_semantics=("arbitrary", "arbitrary"), vmem_limit_bytes=VMEM_LIMIT),
        name="hgrn2_bidirectional_scan",
    )(hq, hv, lff, kf, hq, hv, lfb, kb)


def _attn_kernel(q_ref, kt_ref, v_ref, o_ref, *, ctx):
    tq = q_ref.shape[0]
    s_all = kt_ref.shape[1]
    lane = lax.broadcasted_iota(jnp.int32, (tq, 2 * MLA_V), 1)

    def attend(k0):
        outs = []
        for hh in range(2):
            hs = slice(hh * HEAD_PAD, (hh + 1) * HEAD_PAD)
            s = _dot(q_ref[:, hs], kt_ref[hs, k0:])
            p = jnp.exp2(s - jnp.max(s, axis=-1, keepdims=True))
            l = jnp.sum(p, axis=-1, keepdims=True)
            outs.append(_dot(p.astype(BF16), v_ref[k0:, :]) / l)
        o_ref[...] = jnp.where(lane < MLA_V, outs[0], outs[1]).astype(o_ref.dtype)

    is_ctx = pl.program_id(2) == pl.num_programs(2) - 1
    pl.when(is_ctx)(lambda: attend(s_all - ctx))
    pl.when(jnp.logical_not(is_ctx))(lambda: attend(0))


def _attention(q, kt, v, dims):
    n_batch, seq, ctx = dims[0], dims[1], dims[2]
    s_all = seq + ctx
    tq = TOKEN_TILE
    assert ctx == tq
    pairs = MLA_HEADS // 2
    return pl.pallas_call(
        functools.partial(_attn_kernel, ctx=ctx),
        grid=(n_batch, pairs, s_all // tq),
        in_specs=[
            pl.BlockSpec((None, tq, 2 * HEAD_PAD), lambda b, p, i: (b, i, p)),
            pl.BlockSpec((None, 2 * HEAD_PAD, s_all), lambda b, p, i: (b, p, 0)),
            pl.BlockSpec((None, s_all, 2 * MLA_V), lambda b, p, i: (b, 0, p)),
        ],
        out_specs=pl.BlockSpec((None, tq, 2 * MLA_V), lambda b, p, i: (b, i, p)),
        out_shape=jax.ShapeDtypeStruct((n_batch, s_all, MLA_HEADS * MLA_V), BF16),
        compiler_params=pltpu.CompilerParams(
            dimension_semantics=("arbitrary", "arbitrary", "arbitrary"), vmem_limit_bytes=VMEM_LIMIT),
        name="latent_attention",
    )(q, kt, v)


def _outproj_kernel(x_ref, mod_ref, of_ref, ob_ref, g_ref, att_ref, pu_ref, pprev_ref, pnext_ref,
                    hgain_ref, wp_ref, pscale_ref, wout_ref, o_ref, *, lat_tiles, per_b, seq, ctx):
    i = pl.program_id(0)
    tm = x_ref.shape[0]
    is_lat = i < lat_tiles
    r = jnp.where(is_lat, i % per_b, 0)
    seq_len = jnp.where(is_lat, seq, ctx)
    seq_tiles = jnp.where(is_lat, per_b, ctx // tm)

    o = of_ref[...] + ob_ref[...]
    block_ones = _same_group(HG_W, HG_W, HG_DV, HG_DV).astype(BF16)
    ms = _exact_dot_right(o * o, block_ones) * (1.0 / HG_DV)
    hg = o * lax.rsqrt(ms + EPS) * hgain_ref[...] * g_ref[...]

    h8 = POOL_HALO
    prev = jnp.where(r > 0, pprev_ref[...], 0.0)
    nxt = jnp.where(r < seq_tiles - 1, pnext_ref[...], 0.0)
    u = pu_ref[...]
    ext = jnp.concatenate([prev, u, nxt], axis=0)
    n_ext = tm + 2 * h8

    def shifted(a, delta):
        return pltpu.roll(a, (-delta) % n_ext, 0)

    s2 = ext + shifted(ext, -1)
    s4 = shifted(s2, -1) + shifted(s2, 1)
    s8 = shifted(s4, -2) + shifted(s4, 2)
    s16 = shifted(s8, -4) + shifted(s8, 4)
    pos = r * tm + lax.broadcasted_iota(jnp.int32, (tm, POOL_W), 0)
    lane_g = lax.broadcasted_iota(jnp.int32, (tm, POOL_W), 1) // POOL_CH
    pooled = jnp.zeros((tm, POOL_W), F32)
    for g, (w, sums) in enumerate(zip(POOL_WINDOWS, (s2, s4, s8, s16))):
        lo = jnp.maximum(pos - w // 2, 0)
        hi = jnp.minimum(pos + w - 1 - w // 2, seq_len - 1)
        cnt = (hi - lo + 1).astype(F32)
        pooled = jnp.where(lane_g == g, sums[h8:h8 + tm, :] / cnt - u, pooled)
    pool = _dot(pooled.astype(BF16), wp_ref[...]) * pscale_ref[...]

    mix = jnp.concatenate([hg.astype(BF16), att_ref[...], pool.astype(BF16)], axis=1)
    o_ref[...] = x_ref[...] + mod_ref[5:6, :] * _dot(mix, wout_ref[...])


def _outproj(xs, mod, o_f, o_b, g, att, pu, wts, layer, dims, with_ctx):
    n_batch, seq, ctx, n_lat, n_tokens, d, depth = dims
    tm = TOKEN_TILE
    lat_tiles, per_b = n_lat // tm, seq // tm
    n_out = n_tokens if with_ctx else n_lat
    s_all = seq + ctx
    hb = tm // POOL_HALO

    def b_of(i):
        return jnp.where(i < lat_tiles, i // per_b, i - lat_tiles)

    def r_of(i):
        return jnp.where(i < lat_tiles, i % per_b, per_b)

    def seq_in(width):
        return pl.BlockSpec((None, tm, width), lambda i: (b_of(i), r_of(i), 0))

    def per_layer(*shape):
        return pl.BlockSpec((None,) + shape, lambda i: (layer,) + (0,) * len(shape))

    last_halo = s_all // POOL_HALO - 1
    prev_spec = pl.BlockSpec((None, POOL_HALO, POOL_W),
                             lambda i: (b_of(i), jnp.maximum(r_of(i) * hb - 1, 0), 0))
    next_spec = pl.BlockSpec((None, POOL_HALO, POOL_W),
                             lambda i: (b_of(i), jnp.minimum((r_of(i) + 1) * hb, last_halo), 0))
    return pl.pallas_call(
        functools.partial(_outproj_kernel, lat_tiles=lat_tiles, per_b=per_b, seq=seq, ctx=ctx),
        grid=(n_out // tm,),
        in_specs=[
            pl.BlockSpec((tm, d), lambda i: (i, 0)),
            pl.BlockSpec((None, None, N_MOD, d), lambda i: (layer, jnp.where(i < lat_tiles, i // per_b, n_batch), 0, 0)),
            seq_in(HG_W), seq_in(HG_W), seq_in(HG_W), seq_in(MLA_HEADS * MLA_V), seq_in(POOL_W),
            prev_spec, next_spec,
            per_layer(1, HG_W), per_layer(POOL_W, POOL_W), per_layer(1, POOL_W), per_layer(d, d),
        ],
        out_specs=pl.BlockSpec((tm, d), lambda i: (i, 0)),
        out_shape=jax.ShapeDtypeStruct((n_out, d), F32),
        compiler_params=pltpu.CompilerParams(
            dimension_semantics=("arbitrary",), vmem_limit_bytes=VMEM_LIMIT),
        name="mixer_out_projection",
    )(xs, mod, o_f, o_b, g, att, pu, pu, pu, wts["hg_gain"], wts["w_pool"], wts["pool_scale"], wts["w_out"])


def _rope_tables(seq, ctx):
    rows = seq // GRID_W
    row = np.repeat(np.arange(rows), GRID_W).astype(np.float32)
    col = np.tile(np.arange(GRID_W), rows).astype(np.float32)
    n_freq = MLA_ROPE // 4
    inv_freq = jnp.asarray(ROPE_BASE, F32) ** (-jnp.arange(n_freq, dtype=F32) / n_freq)
    ang = jnp.concatenate([jnp.asarray(row)[:, None] * inv_freq, jnp.asarray(col)[:, None] * inv_freq], axis=-1)
    cos, sin = jnp.cos(ang), jnp.sin(ang)
    half = MLA_ROPE // 2
    zeros = lambda w: jnp.zeros((seq, w), F32)
    cos_t = jnp.concatenate([jnp.ones((seq, MLA_NOPE), F32), cos, cos, zeros(HEAD_PAD - MLA_QK)], axis=1)
    sin_a = jnp.concatenate([zeros(MLA_NOPE), -sin, zeros(HEAD_PAD - MLA_NOPE - half)], axis=1)
    sin_b = jnp.concatenate([zeros(MLA_NOPE + half), sin, zeros(HEAD_PAD - MLA_QK)], axis=1)
    ident = jnp.concatenate([jnp.ones((ctx, MLA_QK), F32), jnp.zeros((ctx, HEAD_PAD - MLA_QK), F32)], axis=1)
    pad0 = jnp.zeros((ctx, HEAD_PAD), F32)
    return (jnp.concatenate([cos_t, ident], axis=0), jnp.concatenate([sin_a, pad0], axis=0),
            jnp.concatenate([sin_b, pad0], axis=0))


def _prepare_weights(p, dims):
    seq, ctx_len, d, depth = dims[1], dims[2], dims[5], dims[6]
    pad_cols = lambda a, n: jnp.pad(a, [(0, 0)] * (a.ndim - 1) + [(0, n)])
    w_in = p["w_in"]
    w_in_p = jnp.concatenate([
        w_in[:, :, :C_POOL], w_in[:, :, C_POOL + MLA_ROPE:],
        jnp.zeros((depth, d, MLA_NOPE), F32), w_in[:, :, C_POOL:C_POOL + MLA_ROPE],
        jnp.zeros((depth, d, HEAD_PAD - MLA_QK), F32)], axis=-1).astype(BF16)
    w_uq = pad_cols(p["mla_w_uq"].reshape(depth, MLA_Q_RANK, MLA_HEADS, MLA_QK), HEAD_PAD - MLA_QK)
    w_ukv = p["mla_w_ukv"].reshape(depth, MLA_KV_RANK, MLA_HEADS, MLA_NOPE + MLA_V)
    cos_t, sin_a, sin_b = _rope_tables(seq, ctx_len)
    eye = jnp.eye(len(POOL_WINDOWS), dtype=F32)
    wts = {
        "w_in": w_in_p,
        "lb_logits": p["hg_lb_logits"].astype(F32),
        "q_a_gain": p["mla_q_a_gain"].reshape(depth, 1, MLA_Q_RANK),
        "w_uq": w_uq.reshape(depth, MLA_Q_RANK, MLA_HEADS * HEAD_PAD).astype(BF16),
        "kv_a_gain": p["mla_kv_a_gain"].reshape(depth, 1, MLA_KV_RANK),
        "w_uk": pad_cols(w_ukv[..., :MLA_NOPE], HEAD_PAD - MLA_NOPE).reshape(
            depth, MLA_KV_RANK, MLA_HEADS * HEAD_PAD).astype(BF16),
        "w_uv": w_ukv[..., MLA_NOPE:].reshape(depth, MLA_KV_RANK, MLA_HEADS * MLA_V).astype(BF16),
        "q_gain": pad_cols(p["mla_q_gain"], HEAD_PAD - MLA_QK).reshape(depth, 1, HEAD_PAD),
        "k_gain": pad_cols(p["mla_k_gain"], HEAD_PAD - MLA_QK).reshape(depth, 1, HEAD_PAD),
        "cos": cos_t, "sin_a": sin_a, "sin_b": sin_b,
        "hg_gain": jnp.tile(p["hg_out_gain"], (1, HG_HEADS)).reshape(depth, 1, HG_W),
        "w_pool": jnp.einsum("gh,lgcd->lgchd", eye, p["pool_w"]).reshape(depth, POOL_W, POOL_W).astype(BF16),
        "pool_scale": p["pool_scale"].reshape(depth, 1, POOL_W),
        "w_out": p["w_out"].astype(BF16),
    }
    ffn = tuple(p[n].astype(BF16) for n in ("ffn1_w_in", "ffn1_w_out", "ffn2_w_in", "ffn2_w_out"))
    return wts, ffn


def kernel(x, c, ctx, c_ctx, w_mod, b_mod, ffn1_w_in, ffn1_w_out, w_in, w_out, hg_lb_logits,
           hg_out_gain, mla_q_a_gain, mla_w_uq, mla_kv_a_gain, mla_w_ukv, mla_q_gain, mla_k_gain,
           pool_w, pool_scale, ffn2_w_in, ffn2_w_out):
    n_batch, seq, d = x.shape
    ctx_len = ctx.shape[1]
    depth = w_mod.shape[0]
    n_lat, n_ctx = n_batch * seq, n_batch * ctx_len
    n_tokens = n_lat + n_ctx
    dims = (n_batch, seq, ctx_len, n_lat, n_tokens, d, depth)
    assert n_batch + 1 <= 8 and seq % FFN_TILE == 0 and n_ctx % FFN_TILE == 0
    wts, (f1_in, f1_out, f2_in, f2_out) = _prepare_weights(dict(
        w_in=w_in, w_out=w_out, hg_lb_logits=hg_lb_logits, hg_out_gain=hg_out_gain,
        mla_q_a_gain=mla_q_a_gain, mla_w_uq=mla_w_uq, mla_kv_a_gain=mla_kv_a_gain, mla_w_ukv=mla_w_ukv,
        mla_q_gain=mla_q_gain, mla_k_gain=mla_k_gain, pool_w=pool_w, pool_scale=pool_scale,
        ffn1_w_in=ffn1_w_in, ffn1_w_out=ffn1_w_out, ffn2_w_in=ffn2_w_in, ffn2_w_out=ffn2_w_out), dims)

    cond = jnp.concatenate([c, c_ctx[None, :], jnp.zeros((8 - n_batch - 1, d), F32)], axis=0)
    mod = _modulation(cond, w_mod, b_mod).reshape(depth, 8, N_MOD, d)

    xs = jnp.concatenate([x.reshape(n_lat, d), ctx.reshape(n_ctx, d)], axis=0)
    for l in range(depth):
        last = l == depth - 1
        xs = _ffn(xs, mod, f1_in, f1_out, l, 0, n_tokens, n_lat, seq, n_batch)
        hq, hv, lff, kf, lfb, kb, g, pu, q, k, v = _inproj(xs, mod, wts, l, dims)
        o_f, o_b = _hgrn(hq, hv, lff, kf, lfb, kb, dims)
        att = _attention(q, k, v, dims)
        xs = _outproj(xs, mod, o_f, o_b, g, att, pu, wts, l, dims, with_ctx=not last)
        xs = _ffn(xs, mod, f2_in, f2_out, l, 6, n_lat if last else n_tokens, n_lat, seq, n_batch)
    return xs.reshape(n_batch, seq, d)
```

```python
import functools

import numpy as np
import jax
import jax.numpy as jnp
from jax import lax
from jax.experimental import pallas as pl
from jax.experimental.pallas import tpu as pltpu

F32 = jnp.float32
BF16 = jnp.bfloat16
EPS = 1e-6
F_MIN = 1e-6
N_MOD = 9
GRID_W = 64
ROPE_BASE = 10000.0
HG_HEADS = 4
HG_DK = 64
HG_DV = 64
HG_W = HG_HEADS * HG_DK
HG_CHUNK = 64
HG_SUB = 16
MLA_HEADS = 8
MLA_Q_RANK = 384
MLA_KV_RANK = 256
MLA_NOPE = 64
MLA_ROPE = 32
MLA_V = 64
MLA_QK = MLA_NOPE + MLA_ROPE
HEAD_PAD = 128
POOL_WINDOWS = (2, 4, 8, 16)
POOL_CH = 64
POOL_W = len(POOL_WINDOWS) * POOL_CH
POOL_HALO = 8
TOKEN_TILE = 256
FFN_TILE = 512
FFN_CHUNK = 256
ATTN_TILE = 1024
ATTN_SUB = 256
LOG2_E = 1.4426950408889634
VMEM_LIMIT = 56 * 1024 * 1024

C_Q, C_FF, C_FB, C_I, C_G = 0, 256, 512, 768, 1024
C_CQ = 1280
C_CKV = C_CQ + MLA_Q_RANK
C_POOL = C_CKV + MLA_KV_RANK
C_KPE = C_POOL + POOL_W
C_KPES = C_KPE + HEAD_PAD
IN_PAD = C_KPES + HEAD_PAD


def _dot(a, b):
    return jnp.dot(a, b, preferred_element_type=F32)


def _dot_nt(a, b):
    return lax.dot_general(a, b, (((1,), (1,)), ((), ())), preferred_element_type=F32)


def _dot_tn(a, b):
    return lax.dot_general(a, b, (((0,), (0,)), ((), ())), preferred_element_type=F32)


def _sigmoid(x):
    return 1.0 / (1.0 + jnp.exp(-x))


def _silu(x):
    return x * _sigmoid(x)


def _split3(x):
    hi = x.astype(BF16)
    r = x - hi.astype(F32)
    mid = r.astype(BF16)
    lo = (r - mid.astype(F32)).astype(BF16)
    return hi, mid, lo


def _exact_dot_left(m_bf16, x):
    hi, mid, lo = _split3(x)
    return _dot(m_bf16, hi) + _dot(m_bf16, mid) + _dot(m_bf16, lo)


def _exact_dot_right(x, m_bf16):
    hi, mid, lo = _split3(x)
    return _dot(hi, m_bf16) + _dot(mid, m_bf16) + _dot(lo, m_bf16)


def _mod_norm(x, shift, scale):
    y = x * lax.rsqrt(jnp.mean(x * x, axis=-1, keepdims=True) + EPS)
    return y * (1.0 + scale) + shift


def _same_group(rows, cols, row_group, col_group):
    r = lax.broadcasted_iota(jnp.int32, (rows, cols), 0) // row_group
    c = lax.broadcasted_iota(jnp.int32, (rows, cols), 1) // col_group
    return r == c


def _mod_kernel(c_ref, w_ref, b_ref, o_ref):
    a = _silu(c_ref[...])
    o_ref[...] = jnp.dot(a, w_ref[...], preferred_element_type=F32,
                         precision=lax.Precision.HIGHEST) + b_ref[...]


def _modulation(cond, w_mod, b_mod):
    depth, d, nd = w_mod.shape
    tn = 1024
    return pl.pallas_call(
        _mod_kernel,
        grid=(depth, nd // tn),
        in_specs=[
            pl.BlockSpec((8, d), lambda l, j: (0, 0)),
            pl.BlockSpec((None, d, tn), lambda l, j: (l, 0, j)),
            pl.BlockSpec((None, 1, tn), lambda l, j: (l, 0, j)),
        ],
        out_specs=pl.BlockSpec((None, 8, tn), lambda l, j: (l, 0, j)),
        out_shape=jax.ShapeDtypeStruct((depth, 8, nd), F32),
        compiler_params=pltpu.CompilerParams(
            dimension_semantics=("arbitrary", "arbitrary"), vmem_limit_bytes=VMEM_LIMIT),
        name="adaln_modulation",
    )(cond, w_mod, b_mod.reshape(depth, 1, nd))


def _ffn_kernel(x_ref, mod_ref, win_ref, wout_ref, o_ref, *, row0, d_ff, chunk):
    x = x_ref[...]
    shift = mod_ref[row0:row0 + 1, :]
    scale = mod_ref[row0 + 1:row0 + 2, :]
    gate = mod_ref[row0 + 2:row0 + 3, :]
    h = _mod_norm(x, shift, scale).astype(BF16)
    acc = None
    for j in range(d_ff // chunk):
        g = _dot(h, win_ref[:, j * chunk:(j + 1) * chunk])
        u = _dot(h, win_ref[:, d_ff + j * chunk:d_ff + (j + 1) * chunk])
        a = (_silu(g) * u).astype(BF16)
        p = _dot(a, wout_ref[j * chunk:(j + 1) * chunk, :])
        acc = p if acc is None else acc + p
    o_ref[...] = x + gate * (0.5 * acc)


def _ffn(xs, mod, w_in, w_out, layer, row0, n_tokens, n_lat, seq, n_batch):
    _, d = xs.shape
    d_ff = w_out.shape[1]
    tm = FFN_TILE
    lat_tiles = n_lat // tm

    def mod_map(i):
        return (layer, jnp.where(i < lat_tiles, (i * tm) // seq, n_batch), 0, 0)

    return pl.pallas_call(
        functools.partial(_ffn_kernel, row0=row0, d_ff=d_ff, chunk=FFN_CHUNK),
        grid=(n_tokens // tm,),
        in_specs=[
            pl.BlockSpec((tm, d), lambda i: (i, 0)),
            pl.BlockSpec((None, None, N_MOD, d), mod_map),
            pl.BlockSpec((None, d, 2 * d_ff), lambda i: (layer, 0, 0)),
            pl.BlockSpec((None, d_ff, d), lambda i: (layer, 0, 0)),
        ],
        out_specs=pl.BlockSpec((tm, d), lambda i: (i, 0)),
        out_shape=jax.ShapeDtypeStruct((n_tokens, d), F32),
        compiler_params=pltpu.CompilerParams(
            dimension_semantics=("arbitrary",), vmem_limit_bytes=VMEM_LIMIT),
        name="swiglu_half_step",
    )(xs, mod, w_in, w_out)


def _pair_norm_rope(x, xs, gc, gs, slot_ones):
    ss = _dot((x * x).astype(BF16), slot_ones)
    return (x * gc + xs * gs) * lax.rsqrt(ss * (1.0 / MLA_QK) + EPS)


def _inproj_kernel(x_ref, mod_ref, w_ref, lbl_ref, qag_ref, wuq_ref, kvag_ref, wuk_ref, wuv_ref,
                   qg_ref, qgs_ref, kg_ref, kgs_ref, cos_ref, sin_ref,
                   hq_ref, hv_ref, lff_ref, kf_ref, lfb_ref, kb_ref, g_ref, pu_ref,
                   q_ref, k_ref, v_ref, *, layer):
    x = x_ref[...]
    h = _mod_norm(x, mod_ref[3:4, :], mod_ref[4:5, :]).astype(BF16)
    y = _dot(h, w_ref[...])

    logits = lbl_ref[...]
    e = jnp.exp(logits - jnp.max(logits, axis=0, keepdims=True))
    p = e / jnp.sum(e, axis=0, keepdims=True)
    lb = jnp.sum(p[:layer + 1], axis=0) - p[0]

    hq_ref[...] = _silu(y[:, C_Q:C_Q + HG_W])
    hv_ref[...] = y[:, C_I:C_I + HG_W]
    g_ref[...] = _silu(y[:, C_G:C_G + HG_W])
    for col, lrow, lf_ref, kk_ref in ((C_FF, 0, lff_ref, kf_ref), (C_FB, 1, lfb_ref, kb_ref)):
        lbv = lb[lrow:lrow + 1, :]
        f = lbv + (1.0 - lbv) * _sigmoid(y[:, col:col + HG_W])
        lf_ref[...] = jnp.log(jnp.clip(f, F_MIN, 1.0))
        kk_ref[...] = 1.0 - f
    pu_ref[...] = y[:, C_POOL:C_POOL + POOL_W]

    wide = MLA_HEADS * HEAD_PAD
    cq = y[:, C_CQ:C_CQ + MLA_Q_RANK]
    cqn = cq * lax.rsqrt(jnp.mean(cq * cq, axis=-1, keepdims=True) + EPS) * qag_ref[...]
    qf = _dot(cqn.astype(BF16), wuq_ref[...])
    ckv = y[:, C_CKV:C_CKV + MLA_KV_RANK]
    ckvn = (ckv * lax.rsqrt(jnp.mean(ckv * ckv, axis=-1, keepdims=True) + EPS) * kvag_ref[...]).astype(BF16)
    kn = _dot(ckvn, wuk_ref[...])
    v_ref[...] = _dot(ckvn, wuv_ref[...]).astype(BF16)
    kpe = y[:, C_KPE:C_KPE + HEAD_PAD]
    kpe2 = jnp.concatenate([kpe, kpe], axis=1)
    kpes = y[:, C_KPES:C_KPES + HEAD_PAD]
    kpes2 = jnp.concatenate([kpes, kpes], axis=1)

    def pair_tables(g_ref, gs_ref, scale):
        gc = g_ref[...] * scale * cos_ref[...]
        gs = gs_ref[...] * scale * sin_ref[...]
        return jnp.concatenate([gc, gc], axis=1), jnp.concatenate([gs, gs], axis=1)

    q_gc, q_gs = pair_tables(qg_ref, qgs_ref, MLA_QK ** -0.5 * LOG2_E)
    k_gc, k_gs = pair_tables(kg_ref, kgs_ref, 1.0)
    slot_ones = _same_group(2 * HEAD_PAD, 2 * HEAD_PAD, HEAD_PAD, HEAD_PAD).astype(BF16)
    for pr in range(MLA_HEADS // 2):
        sl = slice(pr * 2 * HEAD_PAD, (pr + 1) * 2 * HEAD_PAD)
        sls = slice(wide + pr * 2 * HEAD_PAD, wide + (pr + 1) * 2 * HEAD_PAD)
        q_ref[:, sl] = _pair_norm_rope(qf[:, sl], qf[:, sls], q_gc, q_gs, slot_ones).astype(BF16)
        kh = _pair_norm_rope(kn[:, sl] + kpe2, kpes2, k_gc, k_gs, slot_ones)
        k_ref[sl, :] = kh.T.astype(BF16)


def _inproj(xs, mod, wts, layer, dims):
    n_batch, seq, ctx, n_lat, n_tokens, d, depth = dims
    tm = TOKEN_TILE
    lat_tiles, per_b = n_lat // tm, seq // tm
    s_all = seq + ctx

    def b_of(i):
        return jnp.where(i < lat_tiles, i // per_b, i - lat_tiles)

    def r_of(i):
        return jnp.where(i < lat_tiles, i % per_b, per_b)

    def const(*shape):
        return pl.BlockSpec(shape, lambda i: (0,) * len(shape))

    def per_layer(*shape):
        return pl.BlockSpec((None,) + shape, lambda i: (layer,) + (0,) * len(shape))

    def seq_out(width):
        return pl.BlockSpec((None, tm, width), lambda i: (b_of(i), r_of(i), 0))

    rope_spec = pl.BlockSpec((tm, HEAD_PAD), lambda i: (r_of(i), 0))
    f32_out = jax.ShapeDtypeStruct((n_batch, s_all, HG_W), F32)
    wide = MLA_HEADS * HEAD_PAD
    return pl.pallas_call(
        functools.partial(_inproj_kernel, layer=layer),
        grid=(n_tokens // tm,),
        in_specs=[
            pl.BlockSpec((tm, d), lambda i: (i, 0)),
            pl.BlockSpec((None, None, N_MOD, d), lambda i: (layer, jnp.where(i < lat_tiles, i // per_b, n_batch), 0, 0)),
            per_layer(d, IN_PAD),
            const(depth, 2, HG_W),
            per_layer(1, MLA_Q_RANK), per_layer(MLA_Q_RANK, 2 * wide),
            per_layer(1, MLA_KV_RANK), per_layer(MLA_KV_RANK, wide), per_layer(MLA_KV_RANK, MLA_HEADS * MLA_V),
            per_layer(1, HEAD_PAD), per_layer(1, HEAD_PAD), per_layer(1, HEAD_PAD), per_layer(1, HEAD_PAD),
            rope_spec, rope_spec,
        ],
        out_specs=[seq_out(HG_W)] * 8 + [
            seq_out(wide), pl.BlockSpec((None, wide, tm), lambda i: (b_of(i), 0, r_of(i))),
            seq_out(MLA_HEADS * MLA_V)],
        out_shape=[f32_out] * 8 + [
            jax.ShapeDtypeStruct((n_batch, s_all, wide), BF16),
            jax.ShapeDtypeStruct((n_batch, wide, s_all), BF16),
            jax.ShapeDtypeStruct((n_batch, s_all, MLA_HEADS * MLA_V), BF16)],
        compiler_params=pltpu.CompilerParams(
            dimension_semantics=("arbitrary",), vmem_limit_bytes=VMEM_LIMIT),
        name="mixer_in_projection",
    )(xs, mod, wts["w_in"], wts["lb_logits"], wts["q_a_gain"], wts["w_uq"], wts["kv_a_gain"],
      wts["w_uk"], wts["w_uv"], wts["q_gain"], wts["q_gain_s"], wts["k_gain"], wts["k_gain_s"],
      wts["cos"], wts["sin"])


def _hg_chunk(q, k, v, lf, st_ref, rev):
    c, sub = HG_CHUNK, HG_SUB
    n_sub = c // sub
    ri = lax.broadcasted_iota(jnp.int32, (c, c), 0)
    ci = lax.broadcasted_iota(jnp.int32, (c, c), 1)
    tri = ((ci >= ri) if rev else (ci <= ri)).astype(BF16)
    b = _exact_dot_left(tri, lf)
    b_tot = b[0:1, :] if rev else b[c - 1:c, :]
    head_sq = _same_group(HG_W, HG_W, HG_DV, HG_DK)
    head_sub = _same_group(n_sub * sub, HG_W, sub, HG_DK)
    block_ones = head_sq.astype(BF16)

    st = st_ref[...]
    o = _dot_nt((q * jnp.exp(b)).astype(BF16), st.astype(BF16))
    kd = (k * jnp.exp(b_tot - b)).astype(BF16)
    upd = _dot_tn(v.astype(BF16), kd)
    st_ref[...] = st * jnp.exp(b_tot) + jnp.where(head_sq, upd, 0.0)

    def rows(a, i):
        return a[i * sub:(i + 1) * sub, :]

    parts = [[rows(o, i)] for i in range(n_sub)]
    for j in (range(1, n_sub) if rev else range(n_sub - 1)):
        kj, bj, vj = rows(k, j), rows(b, j), rows(v, j)
        ref = bj[0:1, :] if rev else bj[sub - 1:sub, :]
        kh = kj * jnp.exp(ref - bj)
        kh4 = jnp.where(head_sub, jnp.concatenate([kh] * HG_HEADS, axis=0), 0.0).astype(BF16)
        v4 = jnp.where(head_sub, jnp.concatenate([vj] * HG_HEADS, axis=0), 0.0).astype(BF16)
        lo, hi = (0, j * sub) if rev else ((j + 1) * sub, c)
        qt = (q[lo:hi, :] * jnp.exp(b[lo:hi, :] - ref)).astype(BF16)
        att = _dot_nt(qt, kh4)
        oa = _dot(att.astype(BF16), v4)
        for i in range(lo // sub, hi // sub):
            parts[i].append(oa[i * sub - lo:(i + 1) * sub - lo, :])
    g = sub // 2
    cidx = lax.broadcasted_iota(jnp.int32, (sub, HG_W), 0)
    full_keys = range(g, sub) if rev else range(g)
    half_keys = range(g) if rev else range(g, sub)
    hrows = slice(0, g) if rev else slice(g, sub)
    cidx_half = lax.broadcasted_iota(jnp.int32, (g, HG_W), 0) + hrows.start

    def pair_terms(qq, bb, cc, ks, bs, s):
        e = qq * ks * jnp.exp(jnp.minimum(bb - bs, 0.0))
        return jnp.where((cc <= s) if rev else (cc >= s), e, 0.0)

    pieces = []
    for i in range(n_sub):
        qi, ki, bi = rows(q, i), rows(k, i), rows(b, i)
        for s in full_keys:
            pieces.append(pair_terms(qi, bi, cidx, ki[s:s + 1, :], bi[s:s + 1, :], s).astype(BF16))
        halves = [pair_terms(qi[hrows, :], bi[hrows, :], cidx_half, ki[s:s + 1, :], bi[s:s + 1, :], s)
                  for s in half_keys]
        for h0, h1 in zip(halves[0::2], halves[1::2]):
            pieces.append(jnp.concatenate([h0, h1], axis=0).astype(BF16))
    r = _dot(jnp.concatenate(pieces, axis=0), block_ones)
    off = 0
    for i in range(n_sub):
        vi = rows(v, i)
        acc_full = None
        for s in full_keys:
            t = r[off:off + sub, :] * vi[s:s + 1, :]
            acc_full = t if acc_full is None else acc_full + t
            off += sub
        acc_half = None
        for s0, s1 in zip(half_keys[0::2], half_keys[1::2]):
            t = r[off:off + g, :] * vi[s0:s0 + 1, :] + r[off + g:off + sub, :] * vi[s1:s1 + 1, :]
            acc_half = t if acc_half is None else acc_half + t
            off += sub
        lo_half, hi_half = acc_full[:g, :], acc_full[g:, :]
        if rev:
            lo_half = lo_half + acc_half
        else:
            hi_half = hi_half + acc_half
        parts[i].append(jnp.concatenate([lo_half, hi_half], axis=0))
    outs = []
    for plist in parts:
        tot = plist[0]
        for t in plist[1:]:
            tot = tot + t
        outs.append(tot)
    return jnp.concatenate(outs, axis=0)


def _hgrn_kernel(qf_ref, vf_ref, lff_ref, kf_ref, qb_ref, vb_ref, lfb_ref, kb_ref,
                 of_ref, ob_ref, sf_ref, sb_ref, *, tb):
    @pl.when(pl.program_id(1) == 0)
    def _():
        sf_ref[...] = jnp.zeros_like(sf_ref)
        sb_ref[...] = jnp.zeros_like(sb_ref)

    n_chunks = tb // HG_CHUNK

    def body(ci, carry):
        sl = pl.ds(pl.multiple_of(ci * HG_CHUNK, HG_CHUNK), HG_CHUNK)
        of_ref[sl, :] = _hg_chunk(qf_ref[sl, :], kf_ref[sl, :], vf_ref[sl, :], lff_ref[sl, :], sf_ref, False)
        sr = pl.ds(pl.multiple_of((n_chunks - 1 - ci) * HG_CHUNK, HG_CHUNK), HG_CHUNK)
        ob_ref[sr, :] = _hg_chunk(qb_ref[sr, :], kb_ref[sr, :], vb_ref[sr, :], lfb_ref[sr, :], sb_ref, True)
        return carry

    lax.fori_loop(0, n_chunks, body, 0, unroll=True)


def _hgrn(hq, hv, lff, kf, lfb, kb, dims):
    n_batch, seq, ctx = dims[0], dims[1], dims[2]
    tb = TOKEN_TILE
    assert ctx == tb and seq % tb == 0
    per_b = seq // tb
    fwd = pl.BlockSpec((None, tb, HG_W), lambda b, n: (b, jnp.where(n == 0, per_b, n - 1), 0))
    bwd = pl.BlockSpec((None, tb, HG_W), lambda b, n: (b, jnp.where(n == 0, per_b, per_b - n), 0))
    shape = jax.ShapeDtypeStruct(hq.shape, F32)
    return pl.pallas_call(
        functools.partial(_hgrn_kernel, tb=tb),
        grid=(n_batch, per_b + 1),
        in_specs=[fwd, fwd, fwd, fwd, bwd, bwd, bwd, bwd],
        out_specs=[fwd, bwd],
        out_shape=[shape, shape],
        scratch_shapes=[pltpu.VMEM((HG_W, HG_W), F32), pltpu.VMEM((HG_W, HG_W), F32)],
        compiler_params=pltpu.CompilerParams(
            dimension_semantics=("arbitrary", "arbitrary"), vmem_limit_bytes=VMEM_LIMIT),
        name="hgrn2_bidirectional_scan",
    )(hq, hv, lff, kf, hq, hv, lfb, kb)


def _attn_kernel(q_ref, kt_ref, v_ref, *rest):
    o_ref = rest[-1]
    tq = q_ref.shape[0]
    sub = min(ATTN_SUB, tq)
    lane = lax.broadcasted_iota(jnp.int32, (sub, 2 * MLA_V), 1)
    for t in range(tq // sub):
        rows = slice(t * sub, (t + 1) * sub)
        outs = []
        for hh in range(2):
            hs = slice(hh * HEAD_PAD, (hh + 1) * HEAD_PAD)
            s = _dot(q_ref[rows, hs], kt_ref[hs, :])
            p = jnp.exp2(s - jnp.max(s, axis=-1, keepdims=True))
            l = jnp.sum(p, axis=-1, keepdims=True)
            outs.append(_dot(p.astype(BF16), v_ref[...]) / l)
        o_ref[rows, :] = jnp.where(lane < MLA_V, outs[0], outs[1]).astype(o_ref.dtype)


def _attention(q, kt, v, dims, with_ctx):
    n_batch, seq, ctx = dims[0], dims[1], dims[2]
    s_all = seq + ctx
    tq = min(ATTN_TILE, seq)
    assert seq % tq == 0 and seq % ctx == 0
    pairs = MLA_HEADS // 2
    out_shape = jax.ShapeDtypeStruct((n_batch, s_all, MLA_HEADS * MLA_V), BF16)
    params = pltpu.CompilerParams(
        dimension_semantics=("arbitrary", "arbitrary", "arbitrary"), vmem_limit_bytes=VMEM_LIMIT)
    att = pl.pallas_call(
        _attn_kernel,
        grid=(n_batch, pairs, seq // tq),
        in_specs=[
            pl.BlockSpec((None, tq, 2 * HEAD_PAD), lambda b, p, i: (b, i, p)),
            pl.BlockSpec((None, 2 * HEAD_PAD, s_all), lambda b, p, i: (b, p, 0)),
            pl.BlockSpec((None, s_all, 2 * MLA_V), lambda b, p, i: (b, 0, p)),
        ],
        out_specs=pl.BlockSpec((None, tq, 2 * MLA_V), lambda b, p, i: (b, i, p)),
        out_shape=out_shape,
        compiler_params=params,
        name="latent_attention",
    )(q, kt, v)
    if not with_ctx:
        return att
    cb = seq // ctx
    return pl.pallas_call(
        _attn_kernel,
        grid=(n_batch, pairs, 1),
        in_specs=[
            pl.BlockSpec((None, ctx, 2 * HEAD_PAD), lambda b, p, i: (b, cb, p)),
            pl.BlockSpec((None, 2 * HEAD_PAD, ctx), lambda b, p, i: (b, p, cb)),
            pl.BlockSpec((None, ctx, 2 * MLA_V), lambda b, p, i: (b, cb, p)),
            pl.BlockSpec(memory_space=pl.ANY),
        ],
        out_specs=pl.BlockSpec((None, ctx, 2 * MLA_V), lambda b, p, i: (b, cb, p)),
        out_shape=out_shape,
        input_output_aliases={3: 0},
        compiler_params=params,
        name="context_attention",
    )(q, kt, v, att)


def _outproj_kernel(x_ref, mod_ref, of_ref, ob_ref, g_ref, att_ref, pu_ref, pprev_ref, pnext_ref,
                    hgain_ref, wp_ref, pscale_ref, wout_ref, o_ref, *, lat_tiles, per_b, seq, ctx):
    i = pl.program_id(0)
    tm = x_ref.shape[0]
    is_lat = i < lat_tiles
    r = jnp.where(is_lat, i % per_b, 0)
    seq_len = jnp.where(is_lat, seq, ctx)
    seq_tiles = jnp.where(is_lat, per_b, ctx // tm)

    o = of_ref[...] + ob_ref[...]
    block_ones = _same_group(HG_W, HG_W, HG_DV, HG_DV).astype(BF16)
    ms = _exact_dot_right(o * o, block_ones) * (1.0 / HG_DV)
    hg = o * lax.rsqrt(ms + EPS) * hgain_ref[...] * g_ref[...]

    h8 = POOL_HALO
    prev = jnp.where(r > 0, pprev_ref[...], 0.0)
    nxt = jnp.where(r < seq_tiles - 1, pnext_ref[...], 0.0)
    u = pu_ref[...]
    ext = jnp.concatenate([prev, u, nxt], axis=0)
    n_ext = tm + 2 * h8

    def shifted(a, delta):
        return pltpu.roll(a, (-delta) % n_ext, 0)

    s2 = ext + shifted(ext, -1)
    s4 = shifted(s2, -1) + shifted(s2, 1)
    s8 = shifted(s4, -2) + shifted(s4, 2)
    s16 = shifted(s8, -4) + shifted(s8, 4)
    pos = r * tm + lax.broadcasted_iota(jnp.int32, (tm, POOL_W), 0)
    lane_g = lax.broadcasted_iota(jnp.int32, (tm, POOL_W), 1) // POOL_CH
    pooled = jnp.zeros((tm, POOL_W), F32)
    for g, (w, sums) in enumerate(zip(POOL_WINDOWS, (s2, s4, s8, s16))):
        lo = jnp.maximum(pos - w // 2, 0)
        hi = jnp.minimum(pos + w - 1 - w // 2, seq_len - 1)
        cnt = (hi - lo + 1).astype(F32)
        pooled = jnp.where(lane_g == g, sums[h8:h8 + tm, :] / cnt - u, pooled)
    pool = _dot(pooled.astype(BF16), wp_ref[...]) * pscale_ref[...]

    mix = jnp.concatenate([hg.astype(BF16), att_ref[...], pool.astype(BF16)], axis=1)
    o_ref[...] = x_ref[...] + mod_ref[5:6, :] * _dot(mix, wout_ref[...])


def _outproj(xs, mod, o_f, o_b, g, att, pu, wts, layer, dims, with_ctx):
    n_batch, seq, ctx, n_lat, n_tokens, d, depth = dims
    tm = TOKEN_TILE
    lat_tiles, per_b = n_lat // tm, seq // tm
    n_out = n_tokens if with_ctx else n_lat
    s_all = seq + ctx
    hb = tm // POOL_HALO

    def b_of(i):
        return jnp.where(i < lat_tiles, i // per_b, i - lat_tiles)

    def r_of(i):
        return jnp.where(i < lat_tiles, i % per_b, per_b)

    def seq_in(width):
        return pl.BlockSpec((None, tm, width), lambda i: (b_of(i), r_of(i), 0))

    def per_layer(*shape):
        return pl.BlockSpec((None,) + shape, lambda i: (layer,) + (0,) * len(shape))

    last_halo = s_all // POOL_HALO - 1
    prev_spec = pl.BlockSpec((None, POOL_HALO, POOL_W),
                             lambda i: (b_of(i), jnp.maximum(r_of(i) * hb - 1, 0), 0))
    next_spec = pl.BlockSpec((None, POOL_HALO, POOL_W),
                             lambda i: (b_of(i), jnp.minimum((r_of(i) + 1) * hb, last_halo), 0))
    return pl.pallas_call(
        functools.partial(_outproj_kernel, lat_tiles=lat_tiles, per_b=per_b, seq=seq, ctx=ctx),
        grid=(n_out // tm,),
        in_specs=[
            pl.BlockSpec((tm, d), lambda i: (i, 0)),
            pl.BlockSpec((None, None, N_MOD, d), lambda i: (layer, jnp.where(i < lat_tiles, i // per_b, n_batch), 0, 0)),
            seq_in(HG_W), seq_in(HG_W), seq_in(HG_W), seq_in(MLA_HEADS * MLA_V), seq_in(POOL_W),
            prev_spec, next_spec,
            per_layer(1, HG_W), per_layer(POOL_W, POOL_W), per_layer(1, POOL_W), per_layer(d, d),
        ],
        out_specs=pl.BlockSpec((tm, d), lambda i: (i, 0)),
        out_shape=jax.ShapeDtypeStruct((n_out, d), F32),
        compiler_params=pltpu.CompilerParams(
            dimension_semantics=("arbitrary",), vmem_limit_bytes=VMEM_LIMIT),
        name="mixer_out_projection",
    )(xs, mod, o_f, o_b, g, att, pu, pu, pu, wts["hg_gain"], wts["w_pool"], wts["pool_scale"], wts["w_out"])


def _rope_tables(seq, ctx):
    rows = seq // GRID_W
    row = np.repeat(np.arange(rows), GRID_W).astype(np.float32)
    col = np.tile(np.arange(GRID_W), rows).astype(np.float32)
    n_freq = MLA_ROPE // 4
    inv_freq = jnp.asarray(ROPE_BASE, F32) ** (-jnp.arange(n_freq, dtype=F32) / n_freq)
    ang = jnp.concatenate([jnp.asarray(row)[:, None] * inv_freq, jnp.asarray(col)[:, None] * inv_freq], axis=-1)
    cos, sin = jnp.cos(ang), jnp.sin(ang)
    half = MLA_ROPE // 2
    zeros = lambda w: jnp.zeros((seq, w), F32)
    cos_t = jnp.concatenate([jnp.ones((seq, MLA_NOPE), F32), cos, cos, zeros(HEAD_PAD - MLA_QK)], axis=1)
    sin_s = jnp.concatenate([zeros(MLA_NOPE), -sin, sin, zeros(HEAD_PAD - MLA_QK)], axis=1)
    ident = jnp.concatenate([jnp.ones((ctx, MLA_QK), F32), jnp.zeros((ctx, HEAD_PAD - MLA_QK), F32)], axis=1)
    return (jnp.concatenate([cos_t, ident], axis=0),
            jnp.concatenate([sin_s, jnp.zeros((ctx, HEAD_PAD), F32)], axis=0))


def _prepare_weights(p, dims):
    seq, ctx_len, d, depth = dims[1], dims[2], dims[5], dims[6]
    pad_cols = lambda a, n: jnp.pad(a, [(0, 0)] * (a.ndim - 1) + [(0, n)])
    w_in = p["w_in"]
    half = MLA_ROPE // 2

    def slot(a):
        return pad_cols(a, HEAD_PAD - MLA_QK)

    def swapped_slot(a):
        return jnp.concatenate([jnp.zeros_like(a[..., :MLA_NOPE]), a[..., MLA_NOPE + half:],
                                a[..., MLA_NOPE:MLA_NOPE + half], jnp.zeros_like(a[..., :HEAD_PAD - MLA_QK])], axis=-1)

    kpe_cols = jnp.concatenate([jnp.zeros((depth, d, MLA_NOPE), F32), w_in[:, :, C_POOL:C_POOL + MLA_ROPE]], axis=-1)
    w_in_p = jnp.concatenate([
        w_in[:, :, :C_POOL], w_in[:, :, C_POOL + MLA_ROPE:], slot(kpe_cols), swapped_slot(kpe_cols)],
        axis=-1).astype(BF16)
    w_uq4 = p["mla_w_uq"].reshape(depth, MLA_Q_RANK, MLA_HEADS, MLA_QK)
    wide = MLA_HEADS * HEAD_PAD
    w_uq = jnp.concatenate([slot(w_uq4).reshape(depth, MLA_Q_RANK, wide),
                            swapped_slot(w_uq4).reshape(depth, MLA_Q_RANK, wide)], axis=-1)
    w_ukv = p["mla_w_ukv"].reshape(depth, MLA_KV_RANK, MLA_HEADS, MLA_NOPE + MLA_V)
    cos_t, sin_s = _rope_tables(seq, ctx_len)
    eye = jnp.eye(len(POOL_WINDOWS), dtype=F32)
    wts = {
        "w_in": w_in_p,
        "lb_logits": p["hg_lb_logits"].astype(F32),
        "q_a_gain": p["mla_q_a_gain"].reshape(depth, 1, MLA_Q_RANK),
        "w_uq": w_uq.astype(BF16),
        "kv_a_gain": p["mla_kv_a_gain"].reshape(depth, 1, MLA_KV_RANK),
        "w_uk": pad_cols(w_ukv[..., :MLA_NOPE], HEAD_PAD - MLA_NOPE).reshape(
            depth, MLA_KV_RANK, MLA_HEADS * HEAD_PAD).astype(BF16),
        "w_uv": w_ukv[..., MLA_NOPE:].reshape(depth, MLA_KV_RANK, MLA_HEADS * MLA_V).astype(BF16),
        "q_gain": slot(p["mla_q_gain"]).reshape(depth, 1, HEAD_PAD),
        "q_gain_s": swapped_slot(p["mla_q_gain"]).reshape(depth, 1, HEAD_PAD),
        "k_gain": slot(p["mla_k_gain"]).reshape(depth, 1, HEAD_PAD),
        "k_gain_s": swapped_slot(p["mla_k_gain"]).reshape(depth, 1, HEAD_PAD),
        "cos": cos_t, "sin": sin_s,
        "hg_gain": jnp.tile(p["hg_out_gain"], (1, HG_HEADS)).reshape(depth, 1, HG_W),
        "w_pool": jnp.einsum("gh,lgcd->lgchd", eye, p["pool_w"]).reshape(depth, POOL_W, POOL_W).astype(BF16),
        "pool_scale": p["pool_scale"].reshape(depth, 1, POOL_W),
        "w_out": p["w_out"].astype(BF16),
    }
    ffn = tuple(p[n].astype(BF16) for n in ("ffn1_w_in", "ffn1_w_out", "ffn2_w_in", "ffn2_w_out"))
    return wts, ffn


def kernel(x, c, ctx, c_ctx, w_mod, b_mod, ffn1_w_in, ffn1_w_out, w_in, w_out, hg_lb_logits,
           hg_out_gain, mla_q_a_gain, mla_w_uq, mla_kv_a_gain, mla_w_ukv, mla_q_gain, mla_k_gain,
           pool_w, pool_scale, ffn2_w_in, ffn2_w_out):
    n_batch, seq, d = x.shape
    ctx_len = ctx.shape[1]
    depth = w_mod.shape[0]
    n_lat, n_ctx = n_batch * seq, n_batch * ctx_len
    n_tokens = n_lat + n_ctx
    dims = (n_batch, seq, ctx_len, n_lat, n_tokens, d, depth)
    assert n_batch + 1 <= 8 and seq % FFN_TILE == 0 and n_ctx % FFN_TILE == 0
    wts, (f1_in, f1_out, f2_in, f2_out) = _prepare_weights(dict(
        w_in=w_in, w_out=w_out, hg_lb_logits=hg_lb_logits, hg_out_gain=hg_out_gain,
        mla_q_a_gain=mla_q_a_gain, mla_w_uq=mla_w_uq, mla_kv_a_gain=mla_kv_a_gain, mla_w_ukv=mla_w_ukv,
        mla_q_gain=mla_q_gain, mla_k_gain=mla_k_gain, pool_w=pool_w, pool_scale=pool_scale,
        ffn1_w_in=ffn1_w_in, ffn1_w_out=ffn1_w_out, ffn2_w_in=ffn2_w_in, ffn2_w_out=ffn2_w_out), dims)

    cond = jnp.concatenate([c, c_ctx[None, :], jnp.zeros((8 - n_batch - 1, d), F32)], axis=0)
    mod = _modulation(cond, w_mod, b_mod).reshape(depth, 8, N_MOD, d)

    xs = jnp.concatenate([x.reshape(n_lat, d), ctx.reshape(n_ctx, d)], axis=0)
    for l in range(depth):
        last = l == depth - 1
        xs = _ffn(xs, mod, f1_in, f1_out, l, 0, n_tokens, n_lat, seq, n_batch)
        hq, hv, lff, kf, lfb, kb, g, pu, q, k, v = _inproj(xs, mod, wts, l, dims)
        o_f, o_b = _hgrn(hq, hv, lff, kf, lfb, kb, dims)
        att = _attention(q, k, v, dims, with_ctx=not last)
        xs = _outproj(xs, mod, o_f, o_b, g, att, pu, wts, l, dims, with_ctx=not last)
        xs = _ffn(xs, mod, f2_in, f2_out, l, 6, n_lat if last else n_tokens, n_lat, seq, n_batch)
    return xs.reshape(n_batch, seq, d)
```

```python
import functools

import numpy as np
import jax
import jax.numpy as jnp
from jax import lax
from jax.experimental import pallas as pl
from jax.experimental.pallas import tpu as pltpu

F32 = jnp.float32
BF16 = jnp.bfloat16
EPS = 1e-6
F_MIN = 1e-6
N_MOD = 9
GRID_W = 64
ROPE_BASE = 10000.0
HG_HEADS = 4
HG_DK = 64
HG_DV = 64
HG_W = HG_HEADS * HG_DK
HG_CHUNK = 64
HG_SUB = 16
MLA_HEADS = 8
MLA_Q_RANK = 384
MLA_KV_RANK = 256
MLA_NOPE = 64
MLA_ROPE = 32
MLA_V = 64
MLA_QK = MLA_NOPE + MLA_ROPE
HEAD_PAD = 128
POOL_WINDOWS = (2, 4, 8, 16)
POOL_CH = 64
POOL_W = len(POOL_WINDOWS) * POOL_CH
POOL_HALO = max(POOL_WINDOWS) // 2
TOKEN_TILE = 256
FFN_TILE = 512
FFN_CHUNK = 256
ATTN_TILE = 1024
ATTN_SUB = 256
LOG2_E = 1.4426950408889634
MASKED_LOG2 = -1e30
VMEM_LIMIT = 56 * 1024 * 1024
SUBLANES = 8
MOD_COLS = 1024

C_Q, C_FF, C_FB, C_I, C_G = (j * HG_W for j in range(5))
C_CQ = 5 * HG_W
C_CKV = C_CQ + MLA_Q_RANK
C_POOL = C_CKV + MLA_KV_RANK
C_KPE = C_POOL + POOL_W
C_KPES = C_KPE + HEAD_PAD
IN_PAD = C_KPES + HEAD_PAD


def _dot(a, b):
    return jnp.dot(a, b, preferred_element_type=F32)


def _dot_nt(a, b):
    return lax.dot_general(a, b, (((1,), (1,)), ((), ())), preferred_element_type=F32)


def _dot_tn(a, b):
    return lax.dot_general(a, b, (((0,), (0,)), ((), ())), preferred_element_type=F32)


def _sigmoid(x):
    return 1.0 / (1.0 + jnp.exp(-x))


def _silu(x):
    return x * _sigmoid(x)


def _split3(x):
    hi = x.astype(BF16)
    r = x - hi.astype(F32)
    mid = r.astype(BF16)
    lo = (r - mid.astype(F32)).astype(BF16)
    return hi, mid, lo


def _exact_dot_left(m_bf16, x):
    hi, mid, lo = _split3(x)
    return _dot(m_bf16, hi) + _dot(m_bf16, mid) + _dot(m_bf16, lo)


def _exact_dot_right(x, m_bf16):
    hi, mid, lo = _split3(x)
    return _dot(hi, m_bf16) + _dot(mid, m_bf16) + _dot(lo, m_bf16)


def _mod_norm(x, shift, scale):
    y = x * lax.rsqrt(jnp.mean(x * x, axis=-1, keepdims=True) + EPS)
    return y * (1.0 + scale) + shift


def _same_group(rows, cols, row_group, col_group):
    r = lax.broadcasted_iota(jnp.int32, (rows, cols), 0) // row_group
    c = lax.broadcasted_iota(jnp.int32, (rows, cols), 1) // col_group
    return r == c


def _mod_kernel(c_ref, w_ref, b_ref, o_ref):
    a = _silu(c_ref[...])
    o_ref[...] = jnp.dot(a, w_ref[...], preferred_element_type=F32,
                         precision=lax.Precision.HIGHEST) + b_ref[...]


def _modulation(cond, w_mod, b_mod):
    depth, d, nd = w_mod.shape
    tn = MOD_COLS
    return pl.pallas_call(
        _mod_kernel,
        grid=(depth, nd // tn),
        in_specs=[
            pl.BlockSpec((SUBLANES, d), lambda l, j: (0, 0)),
            pl.BlockSpec((None, d, tn), lambda l, j: (l, 0, j)),
            pl.BlockSpec((None, 1, tn), lambda l, j: (l, 0, j)),
        ],
        out_specs=pl.BlockSpec((None, SUBLANES, tn), lambda l, j: (l, 0, j)),
        out_shape=jax.ShapeDtypeStruct((depth, SUBLANES, nd), F32),
        compiler_params=pltpu.CompilerParams(
            dimension_semantics=("arbitrary", "arbitrary"), vmem_limit_bytes=VMEM_LIMIT),
        name="adaln_modulation",
    )(cond, w_mod, b_mod.reshape(depth, 1, nd))


def _ffn_kernel(x_ref, mod_ref, win_ref, wout_ref, o_ref, *, row0, d_ff, chunk):
    x = x_ref[...]
    shift = mod_ref[row0:row0 + 1, :]
    scale = mod_ref[row0 + 1:row0 + 2, :]
    gate = mod_ref[row0 + 2:row0 + 3, :]
    h = _mod_norm(x, shift, scale).astype(BF16)
    acc = None
    for j in range(d_ff // chunk):
        g = _dot(h, win_ref[:, j * chunk:(j + 1) * chunk])
        u = _dot(h, win_ref[:, d_ff + j * chunk:d_ff + (j + 1) * chunk])
        a = (_silu(g) * u).astype(BF16)
        p = _dot(a, wout_ref[j * chunk:(j + 1) * chunk, :])
        acc = p if acc is None else acc + p
    o_ref[...] = x + gate * (0.5 * acc)


def _ffn(xs, mod, w_in, w_out, layer, row0, n_tokens, n_lat, seq, n_batch):
    _, d = xs.shape
    d_ff = w_out.shape[1]
    tm = FFN_TILE
    lat_tiles = n_lat // tm

    def mod_map(i):
        return (layer, jnp.where(i < lat_tiles, (i * tm) // seq, n_batch), 0, 0)

    return pl.pallas_call(
        functools.partial(_ffn_kernel, row0=row0, d_ff=d_ff, chunk=FFN_CHUNK),
        grid=(n_tokens // tm,),
        in_specs=[
            pl.BlockSpec((tm, d), lambda i: (i, 0)),
            pl.BlockSpec((None, None, N_MOD, d), mod_map),
            pl.BlockSpec((None, d, 2 * d_ff), lambda i: (layer, 0, 0)),
            pl.BlockSpec((None, d_ff, d), lambda i: (layer, 0, 0)),
        ],
        out_specs=pl.BlockSpec((tm, d), lambda i: (i, 0)),
        out_shape=jax.ShapeDtypeStruct((n_tokens, d), F32),
        compiler_params=pltpu.CompilerParams(
            dimension_semantics=("arbitrary",), vmem_limit_bytes=VMEM_LIMIT),
        name="swiglu_half_step",
    )(xs, mod, w_in, w_out)


def _pair_norm_rope(x, xs, gc, gs, slot_ones):
    ss = _dot((x * x).astype(BF16), slot_ones)
    return (x * gc + xs * gs) * lax.rsqrt(ss * (1.0 / MLA_QK) + EPS)


def _inproj_kernel(x_ref, mod_ref, w_ref, lbl_ref, qag_ref, wuq_ref, kvag_ref, wuk_ref, wuv_ref,
                   qg_ref, qgs_ref, kg_ref, kgs_ref, cos_ref, sin_ref,
                   hq_ref, hv_ref, lff_ref, kf_ref, lfb_ref, kb_ref, g_ref, pu_ref,
                   q_ref, k_ref, v_ref, *, layer):
    x = x_ref[...]
    h = _mod_norm(x, mod_ref[3:4, :], mod_ref[4:5, :]).astype(BF16)
    y = _dot(h, w_ref[...])

    logits = lbl_ref[...]
    e = jnp.exp(logits - jnp.max(logits, axis=0, keepdims=True))
    p = e / jnp.sum(e, axis=0, keepdims=True)
    lb = jnp.sum(p[:layer + 1], axis=0) - p[0]

    hq_ref[...] = _silu(y[:, C_Q:C_Q + HG_W])
    hv_ref[...] = y[:, C_I:C_I + HG_W]
    g_ref[...] = _silu(y[:, C_G:C_G + HG_W])
    for col, lrow, lf_ref, kk_ref in ((C_FF, 0, lff_ref, kf_ref), (C_FB, 1, lfb_ref, kb_ref)):
        lbv = lb[lrow:lrow + 1, :]
        f = lbv + (1.0 - lbv) * _sigmoid(y[:, col:col + HG_W])
        lf_ref[...] = jnp.log(jnp.clip(f, F_MIN, 1.0))
        kk_ref[...] = 1.0 - f
    pu_ref[...] = y[:, C_POOL:C_POOL + POOL_W]

    wide = MLA_HEADS * HEAD_PAD
    cq = y[:, C_CQ:C_CQ + MLA_Q_RANK]
    cqn = cq * lax.rsqrt(jnp.mean(cq * cq, axis=-1, keepdims=True) + EPS) * qag_ref[...]
    qf = _dot(cqn.astype(BF16), wuq_ref[...])
    ckv = y[:, C_CKV:C_CKV + MLA_KV_RANK]
    ckvn = (ckv * lax.rsqrt(jnp.mean(ckv * ckv, axis=-1, keepdims=True) + EPS) * kvag_ref[...]).astype(BF16)
    kn = _dot(ckvn, wuk_ref[...])
    v_ref[...] = _dot(ckvn, wuv_ref[...]).astype(BF16)
    kpe = y[:, C_KPE:C_KPE + HEAD_PAD]
    kpe2 = jnp.concatenate([kpe, kpe], axis=1)
    kpes = y[:, C_KPES:C_KPES + HEAD_PAD]
    kpes2 = jnp.concatenate([kpes, kpes], axis=1)

    def pair_tables(g_ref, gs_ref, scale):
        gc = g_ref[...] * scale * cos_ref[...]
        gs = gs_ref[...] * scale * sin_ref[...]
        return jnp.concatenate([gc, gc], axis=1), jnp.concatenate([gs, gs], axis=1)

    q_gc, q_gs = pair_tables(qg_ref, qgs_ref, MLA_QK ** -0.5 * LOG2_E)
    k_gc, k_gs = pair_tables(kg_ref, kgs_ref, 1.0)
    slot_ones = _same_group(2 * HEAD_PAD, 2 * HEAD_PAD, HEAD_PAD, HEAD_PAD).astype(BF16)
    for pr in range(MLA_HEADS // 2):
        sl = slice(pr * 2 * HEAD_PAD, (pr + 1) * 2 * HEAD_PAD)
        sls = slice(wide + pr * 2 * HEAD_PAD, wide + (pr + 1) * 2 * HEAD_PAD)
        q_ref[:, sl] = _pair_norm_rope(qf[:, sl], qf[:, sls], q_gc, q_gs, slot_ones).astype(BF16)
        kh = _pair_norm_rope(kn[:, sl] + kpe2, kpes2, k_gc, k_gs, slot_ones)
        k_ref[sl, :] = kh.T.astype(BF16)


def _inproj(xs, mod, wts, layer, dims):
    n_batch, seq, ctx, n_lat, n_tokens, d, depth = dims
    tm = TOKEN_TILE
    lat_tiles, per_b = n_lat // tm, seq // tm
    s_all = seq + ctx

    def b_of(i):
        return jnp.where(i < lat_tiles, i // per_b, i - lat_tiles)

    def r_of(i):
        return jnp.where(i < lat_tiles, i % per_b, per_b)

    def const(*shape):
        return pl.BlockSpec(shape, lambda i: (0,) * len(shape))

    def per_layer(*shape):
        return pl.BlockSpec((None,) + shape, lambda i: (layer,) + (0,) * len(shape))

    def seq_out(width):
        return pl.BlockSpec((None, tm, width), lambda i: (b_of(i), r_of(i), 0))

    rope_spec = pl.BlockSpec((tm, HEAD_PAD), lambda i: (r_of(i), 0))
    f32_out = jax.ShapeDtypeStruct((n_batch, s_all, HG_W), F32)
    wide = MLA_HEADS * HEAD_PAD
    return pl.pallas_call(
        functools.partial(_inproj_kernel, layer=layer),
        grid=(n_tokens // tm,),
        in_specs=[
            pl.BlockSpec((tm, d), lambda i: (i, 0)),
            pl.BlockSpec((None, None, N_MOD, d), lambda i: (layer, jnp.where(i < lat_tiles, i // per_b, n_batch), 0, 0)),
            per_layer(d, IN_PAD),
            const(depth, 2, HG_W),
            per_layer(1, MLA_Q_RANK), per_layer(MLA_Q_RANK, 2 * wide),
            per_layer(1, MLA_KV_RANK), per_layer(MLA_KV_RANK, wide), per_layer(MLA_KV_RANK, MLA_HEADS * MLA_V),
            per_layer(1, HEAD_PAD), per_layer(1, HEAD_PAD), per_layer(1, HEAD_PAD), per_layer(1, HEAD_PAD),
            rope_spec, rope_spec,
        ],
        out_specs=[seq_out(HG_W)] * 8 + [
            seq_out(wide), pl.BlockSpec((None, wide, tm), lambda i: (b_of(i), 0, r_of(i))),
            seq_out(MLA_HEADS * MLA_V)],
        out_shape=[f32_out] * 8 + [
            jax.ShapeDtypeStruct((n_batch, s_all, wide), BF16),
            jax.ShapeDtypeStruct((n_batch, wide, s_all), BF16),
            jax.ShapeDtypeStruct((n_batch, s_all, MLA_HEADS * MLA_V), BF16)],
        compiler_params=pltpu.CompilerParams(
            dimension_semantics=("arbitrary",), vmem_limit_bytes=VMEM_LIMIT),
        name="mixer_in_projection",
    )(xs, mod, wts["w_in"], wts["lb_logits"], wts["q_a_gain"], wts["w_uq"], wts["kv_a_gain"],
      wts["w_uk"], wts["w_uv"], wts["q_gain"], wts["q_gain_s"], wts["k_gain"], wts["k_gain_s"],
      wts["cos"], wts["sin"])


def _hg_chunk(q, k, v, lf, st_ref, rev):
    c, sub = HG_CHUNK, HG_SUB
    n_sub = c // sub
    ri = lax.broadcasted_iota(jnp.int32, (c, c), 0)
    ci = lax.broadcasted_iota(jnp.int32, (c, c), 1)
    tri = ((ci >= ri) if rev else (ci <= ri)).astype(BF16)
    b = _exact_dot_left(tri, lf) * LOG2_E
    b_tot = b[0:1, :] if rev else b[c - 1:c, :]
    head_sq = _same_group(HG_W, HG_W, HG_DV, HG_DK)
    head_sub = _same_group(n_sub * sub, HG_W, sub, HG_DK)
    block_ones = head_sq.astype(BF16)

    st = st_ref[...]
    o = _dot_nt((q * jnp.exp2(b)).astype(BF16), st.astype(BF16))
    kd = (k * jnp.exp2(b_tot - b)).astype(BF16)
    upd = _dot_tn(v.astype(BF16), kd)
    st_ref[...] = st * jnp.exp2(b_tot) + jnp.where(head_sq, upd, 0.0)

    def rows(a, i):
        return a[i * sub:(i + 1) * sub, :]

    parts = [[rows(o, i)] for i in range(n_sub)]
    for j in (range(1, n_sub) if rev else range(n_sub - 1)):
        kj, bj, vj = rows(k, j), rows(b, j), rows(v, j)
        ref = bj[0:1, :] if rev else bj[sub - 1:sub, :]
        kh = kj * jnp.exp2(ref - bj)
        kh4 = jnp.where(head_sub, jnp.concatenate([kh] * HG_HEADS, axis=0), 0.0).astype(BF16)
        v4 = jnp.where(head_sub, jnp.concatenate([vj] * HG_HEADS, axis=0), 0.0).astype(BF16)
        lo, hi = (0, j * sub) if rev else ((j + 1) * sub, c)
        qt = (q[lo:hi, :] * jnp.exp2(b[lo:hi, :] - ref)).astype(BF16)
        att = _dot_nt(qt, kh4)
        oa = _dot(att.astype(BF16), v4)
        for i in range(lo // sub, hi // sub):
            parts[i].append(oa[i * sub - lo:(i + 1) * sub - lo, :])
    g = sub // 2
    cidx = lax.broadcasted_iota(jnp.int32, (sub, HG_W), 0)
    full_keys = range(g, sub) if rev else range(g)
    half_keys = range(g) if rev else range(g, sub)
    hrows = slice(0, g) if rev else slice(g, sub)
    cidx_half = lax.broadcasted_iota(jnp.int32, (g, HG_W), 0) + hrows.start

    def pair_terms(qq, bb, cc, ks, bs, s):
        keep = (cc <= s) if rev else (cc >= s)
        return qq * ks * jnp.exp2(jnp.where(keep, bb - bs, MASKED_LOG2))

    pieces = []
    for i in range(n_sub):
        qi, ki, bi = rows(q, i), rows(k, i), rows(b, i)
        for s in full_keys:
            pieces.append(pair_terms(qi, bi, cidx, ki[s:s + 1, :], bi[s:s + 1, :], s).astype(BF16))
        halves = [pair_terms(qi[hrows, :], bi[hrows, :], cidx_half, ki[s:s + 1, :], bi[s:s + 1, :], s)
                  for s in half_keys]
        for h0, h1 in zip(halves[0::2], halves[1::2]):
            pieces.append(jnp.concatenate([h0, h1], axis=0).astype(BF16))
    r = _dot(jnp.concatenate(pieces, axis=0), block_ones)
    off = 0
    for i in range(n_sub):
        vi = rows(v, i)
        acc_full = None
        for s in full_keys:
            t = r[off:off + sub, :] * vi[s:s + 1, :]
            acc_full = t if acc_full is None else acc_full + t
            off += sub
        acc_half = None
        for s0, s1 in zip(half_keys[0::2], half_keys[1::2]):
            t = r[off:off + g, :] * vi[s0:s0 + 1, :] + r[off + g:off + sub, :] * vi[s1:s1 + 1, :]
            acc_half = t if acc_half is None else acc_half + t
            off += sub
        lo_half, hi_half = acc_full[:g, :], acc_full[g:, :]
        if rev:
            lo_half = lo_half + acc_half
        else:
            hi_half = hi_half + acc_half
        parts[i].append(jnp.concatenate([lo_half, hi_half], axis=0))
    outs = []
    for plist in parts:
        tot = plist[0]
        for t in plist[1:]:
            tot = tot + t
        outs.append(tot)
    return jnp.concatenate(outs, axis=0)


def _hgrn_kernel(qf_ref, vf_ref, lff_ref, kf_ref, qb_ref, vb_ref, lfb_ref, kb_ref,
                 of_ref, ob_ref, sf_ref, sb_ref, *, tb):
    @pl.when(pl.program_id(1) == 0)
    def _():
        sf_ref[...] = jnp.zeros_like(sf_ref)
        sb_ref[...] = jnp.zeros_like(sb_ref)

    n_chunks = tb // HG_CHUNK

    def body(ci, carry):
        sl = pl.ds(pl.multiple_of(ci * HG_CHUNK, HG_CHUNK), HG_CHUNK)
        of_ref[sl, :] = _hg_chunk(qf_ref[sl, :], kf_ref[sl, :], vf_ref[sl, :], lff_ref[sl, :], sf_ref, False)
        sr = pl.ds(pl.multiple_of((n_chunks - 1 - ci) * HG_CHUNK, HG_CHUNK), HG_CHUNK)
        ob_ref[sr, :] = _hg_chunk(qb_ref[sr, :], kb_ref[sr, :], vb_ref[sr, :], lfb_ref[sr, :], sb_ref, True)
        return carry

    lax.fori_loop(0, n_chunks, body, 0, unroll=True)


def _hgrn(hq, hv, lff, kf, lfb, kb, dims):
    n_batch, seq, ctx = dims[0], dims[1], dims[2]
    tb = TOKEN_TILE
    assert ctx == tb and seq % tb == 0
    per_b = seq // tb
    fwd = pl.BlockSpec((None, tb, HG_W), lambda b, n: (b, jnp.where(n == 0, per_b, n - 1), 0))
    bwd = pl.BlockSpec((None, tb, HG_W), lambda b, n: (b, jnp.where(n == 0, per_b, per_b - n), 0))
    shape = jax.ShapeDtypeStruct(hq.shape, F32)
    return pl.pallas_call(
        functools.partial(_hgrn_kernel, tb=tb),
        grid=(n_batch, per_b + 1),
        in_specs=[fwd, fwd, fwd, fwd, bwd, bwd, bwd, bwd],
        out_specs=[fwd, bwd],
        out_shape=[shape, shape],
        scratch_shapes=[pltpu.VMEM((HG_W, HG_W), F32), pltpu.VMEM((HG_W, HG_W), F32)],
        compiler_params=pltpu.CompilerParams(
            dimension_semantics=("arbitrary", "arbitrary"), vmem_limit_bytes=VMEM_LIMIT),
        name="hgrn2_bidirectional_scan",
    )(hq, hv, lff, kf, hq, hv, lfb, kb)


def _attn_kernel(q_ref, kt_ref, v_ref, *rest):
    o_ref = rest[-1]
    tq = q_ref.shape[0]
    sub = min(ATTN_SUB, tq)
    lane = lax.broadcasted_iota(jnp.int32, (sub, 2 * MLA_V), 1)
    for t in range(tq // sub):
        rows = slice(t * sub, (t + 1) * sub)
        outs = []
        for hh in range(2):
            hs = slice(hh * HEAD_PAD, (hh + 1) * HEAD_PAD)
            s = _dot(q_ref[rows, hs], kt_ref[hs, :])
            p = jnp.exp2(s - jnp.max(s, axis=-1, keepdims=True))
            l = jnp.sum(p, axis=-1, keepdims=True)
            outs.append(_dot(p.astype(BF16), v_ref[...]) / l)
        o_ref[rows, :] = jnp.where(lane < MLA_V, outs[0], outs[1]).astype(o_ref.dtype)


def _attention(q, kt, v, dims, with_ctx):
    n_batch, seq, ctx = dims[0], dims[1], dims[2]
    s_all = seq + ctx
    tq = min(ATTN_TILE, seq)
    assert seq % tq == 0 and seq % ctx == 0
    pairs = MLA_HEADS // 2
    out_shape = jax.ShapeDtypeStruct((n_batch, s_all, MLA_HEADS * MLA_V), BF16)
    params = pltpu.CompilerParams(
        dimension_semantics=("arbitrary", "arbitrary", "arbitrary"), vmem_limit_bytes=VMEM_LIMIT)
    att = pl.pallas_call(
        _attn_kernel,
        grid=(n_batch, pairs, seq // tq),
        in_specs=[
            pl.BlockSpec((None, tq, 2 * HEAD_PAD), lambda b, p, i: (b, i, p)),
            pl.BlockSpec((None, 2 * HEAD_PAD, s_all), lambda b, p, i: (b, p, 0)),
            pl.BlockSpec((None, s_all, 2 * MLA_V), lambda b, p, i: (b, 0, p)),
        ],
        out_specs=pl.BlockSpec((None, tq, 2 * MLA_V), lambda b, p, i: (b, i, p)),
        out_shape=out_shape,
        compiler_params=params,
        name="latent_attention",
    )(q, kt, v)
    if not with_ctx:
        return att
    cb = seq // ctx
    return pl.pallas_call(
        _attn_kernel,
        grid=(n_batch, pairs, 1),
        in_specs=[
            pl.BlockSpec((None, ctx, 2 * HEAD_PAD), lambda b, p, i: (b, cb, p)),
            pl.BlockSpec((None, 2 * HEAD_PAD, ctx), lambda b, p, i: (b, p, cb)),
            pl.BlockSpec((None, ctx, 2 * MLA_V), lambda b, p, i: (b, cb, p)),
            pl.BlockSpec(memory_space=pl.ANY),
        ],
        out_specs=pl.BlockSpec((None, ctx, 2 * MLA_V), lambda b, p, i: (b, cb, p)),
        out_shape=out_shape,
        input_output_aliases={3: 0},
        compiler_params=params,
        name="context_attention",
    )(q, kt, v, att)


def _outproj_kernel(x_ref, mod_ref, of_ref, ob_ref, g_ref, att_ref, pu_ref, pprev_ref, pnext_ref,
                    hgain_ref, wp_ref, pscale_ref, wout_ref, o_ref, *, lat_tiles, per_b, seq, ctx):
    i = pl.program_id(0)
    tm = x_ref.shape[0]
    is_lat = i < lat_tiles
    r = jnp.where(is_lat, i % per_b, 0)
    seq_len = jnp.where(is_lat, seq, ctx)
    seq_tiles = jnp.where(is_lat, per_b, ctx // tm)

    o = of_ref[...] + ob_ref[...]
    block_ones = _same_group(HG_W, HG_W, HG_DV, HG_DV).astype(BF16)
    ms = _exact_dot_right(o * o, block_ones) * (1.0 / HG_DV)
    hg = o * lax.rsqrt(ms + EPS) * hgain_ref[...] * g_ref[...]

    h8 = POOL_HALO
    prev = jnp.where(r > 0, pprev_ref[...], 0.0)
    nxt = jnp.where(r < seq_tiles - 1, pnext_ref[...], 0.0)
    u = pu_ref[...]
    ext = jnp.concatenate([prev, u, nxt], axis=0)
    n_ext = tm + 2 * h8

    def shifted(a, delta):
        return pltpu.roll(a, (-delta) % n_ext, 0)

    s2 = ext + shifted(ext, -1)
    s4 = shifted(s2, -1) + shifted(s2, 1)
    s8 = shifted(s4, -2) + shifted(s4, 2)
    s16 = shifted(s8, -4) + shifted(s8, 4)
    pos = r * tm + lax.broadcasted_iota(jnp.int32, (tm, POOL_W), 0)
    lane_g = lax.broadcasted_iota(jnp.int32, (tm, POOL_W), 1) // POOL_CH
    def by_group(values):
        out = values[-1]
        for g in range(len(values) - 2, -1, -1):
            out = jnp.where(lane_g == g, values[g], out)
        return out

    back = by_group([w // 2 for w in POOL_WINDOWS])
    ahead = by_group([w - 1 - w // 2 for w in POOL_WINDOWS])
    cnt = jnp.minimum(pos + ahead, seq_len - 1) - jnp.maximum(pos - back, 0) + 1
    sums = by_group([a[h8:h8 + tm, :] for a in (s2, s4, s8, s16)])
    pooled = sums / cnt.astype(F32) - u
    pool = _dot(pooled.astype(BF16), wp_ref[...]) * pscale_ref[...]

    mix = jnp.concatenate([hg.astype(BF16), att_ref[...], pool.astype(BF16)], axis=1)
    o_ref[...] = x_ref[...] + mod_ref[5:6, :] * _dot(mix, wout_ref[...])


def _outproj(xs, mod, o_f, o_b, g, att, pu, wts, layer, dims, with_ctx):
    n_batch, seq, ctx, n_lat, n_tokens, d, depth = dims
    tm = TOKEN_TILE
    lat_tiles, per_b = n_lat // tm, seq // tm
    n_out = n_tokens if with_ctx else n_lat
    s_all = seq + ctx
    hb = tm // POOL_HALO

    def b_of(i):
        return jnp.where(i < lat_tiles, i // per_b, i - lat_tiles)

    def r_of(i):
        return jnp.where(i < lat_tiles, i % per_b, per_b)

    def seq_in(width):
        return pl.BlockSpec((None, tm, width), lambda i: (b_of(i), r_of(i), 0))

    def per_layer(*shape):
        return pl.BlockSpec((None,) + shape, lambda i: (layer,) + (0,) * len(shape))

    last_halo = s_all // POOL_HALO - 1
    prev_spec = pl.BlockSpec((None, POOL_HALO, POOL_W),
                             lambda i: (b_of(i), jnp.maximum(r_of(i) * hb - 1, 0), 0))
    next_spec = pl.BlockSpec((None, POOL_HALO, POOL_W),
                             lambda i: (b_of(i), jnp.minimum((r_of(i) + 1) * hb, last_halo), 0))
    return pl.pallas_call(
        functools.partial(_outproj_kernel, lat_tiles=lat_tiles, per_b=per_b, seq=seq, ctx=ctx),
        grid=(n_out // tm,),
        in_specs=[
            pl.BlockSpec((tm, d), lambda i: (i, 0)),
            pl.BlockSpec((None, None, N_MOD, d), lambda i: (layer, jnp.where(i < lat_tiles, i // per_b, n_batch), 0, 0)),
            seq_in(HG_W), seq_in(HG_W), seq_in(HG_W), seq_in(MLA_HEADS * MLA_V), seq_in(POOL_W),
            prev_spec, next_spec,
            per_layer(1, HG_W), per_layer(POOL_W, POOL_W), per_layer(1, POOL_W), per_layer(d, d),
        ],
        out_specs=pl.BlockSpec((tm, d), lambda i: (i, 0)),
        out_shape=jax.ShapeDtypeStruct((n_out, d), F32),
        compiler_params=pltpu.CompilerParams(
            dimension_semantics=("arbitrary",), vmem_limit_bytes=VMEM_LIMIT),
        name="mixer_out_projection",
    )(xs, mod, o_f, o_b, g, att, pu, pu, pu, wts["hg_gain"], wts["w_pool"], wts["pool_scale"], wts["w_out"])


def _rope_tables(seq, ctx):
    rows = seq // GRID_W
    row = np.repeat(np.arange(rows), GRID_W).astype(np.float32)
    col = np.tile(np.arange(GRID_W), rows).astype(np.float32)
    n_freq = MLA_ROPE // 4
    inv_freq = jnp.asarray(ROPE_BASE, F32) ** (-jnp.arange(n_freq, dtype=F32) / n_freq)
    ang = jnp.concatenate([jnp.asarray(row)[:, None] * inv_freq, jnp.asarray(col)[:, None] * inv_freq], axis=-1)
    cos, sin = jnp.cos(ang), jnp.sin(ang)
    half = MLA_ROPE // 2
    zeros = lambda w: jnp.zeros((seq, w), F32)
    cos_t = jnp.concatenate([jnp.ones((seq, MLA_NOPE), F32), cos, cos, zeros(HEAD_PAD - MLA_QK)], axis=1)
    sin_s = jnp.concatenate([zeros(MLA_NOPE), -sin, sin, zeros(HEAD_PAD - MLA_QK)], axis=1)
    ident = jnp.concatenate([jnp.ones((ctx, MLA_QK), F32), jnp.zeros((ctx, HEAD_PAD - MLA_QK), F32)], axis=1)
    return (jnp.concatenate([cos_t, ident], axis=0),
            jnp.concatenate([sin_s, jnp.zeros((ctx, HEAD_PAD), F32)], axis=0))


def _prepare_weights(p, dims):
    seq, ctx_len, d, depth = dims[1], dims[2], dims[5], dims[6]
    pad_cols = lambda a, n: jnp.pad(a, [(0, 0)] * (a.ndim - 1) + [(0, n)])
    w_in = p["w_in"]
    half = MLA_ROPE // 2

    def slot(a):
        return pad_cols(a, HEAD_PAD - MLA_QK)

    def swapped_slot(a):
        return jnp.concatenate([jnp.zeros_like(a[..., :MLA_NOPE]), a[..., MLA_NOPE + half:],
                                a[..., MLA_NOPE:MLA_NOPE + half], jnp.zeros_like(a[..., :HEAD_PAD - MLA_QK])], axis=-1)

    kpe_cols = jnp.concatenate([jnp.zeros((depth, d, MLA_NOPE), F32), w_in[:, :, C_POOL:C_POOL + MLA_ROPE]], axis=-1)
    w_in_p = jnp.concatenate([
        w_in[:, :, :C_POOL], w_in[:, :, C_POOL + MLA_ROPE:], slot(kpe_cols), swapped_slot(kpe_cols)],
        axis=-1).astype(BF16)
    w_uq4 = p["mla_w_uq"].reshape(depth, MLA_Q_RANK, MLA_HEADS, MLA_QK)
    wide = MLA_HEADS * HEAD_PAD
    w_uq = jnp.concatenate([slot(w_uq4).reshape(depth, MLA_Q_RANK, wide),
                            swapped_slot(w_uq4).reshape(depth, MLA_Q_RANK, wide)], axis=-1)
    w_ukv = p["mla_w_ukv"].reshape(depth, MLA_KV_RANK, MLA_HEADS, MLA_NOPE + MLA_V)
    cos_t, sin_s = _rope_tables(seq, ctx_len)
    eye = jnp.eye(len(POOL_WINDOWS), dtype=F32)
    wts = {
        "w_in": w_in_p,
        "lb_logits": p["hg_lb_logits"].astype(F32),
        "q_a_gain": p["mla_q_a_gain"].reshape(depth, 1, MLA_Q_RANK),
        "w_uq": w_uq.astype(BF16),
        "kv_a_gain": p["mla_kv_a_gain"].reshape(depth, 1, MLA_KV_RANK),
        "w_uk": pad_cols(w_ukv[..., :MLA_NOPE], HEAD_PAD - MLA_NOPE).reshape(
            depth, MLA_KV_RANK, MLA_HEADS * HEAD_PAD).astype(BF16),
        "w_uv": w_ukv[..., MLA_NOPE:].reshape(depth, MLA_KV_RANK, MLA_HEADS * MLA_V).astype(BF16),
        "q_gain": slot(p["mla_q_gain"]).reshape(depth, 1, HEAD_PAD),
        "q_gain_s": swapped_slot(p["mla_q_gain"]).reshape(depth, 1, HEAD_PAD),
        "k_gain": slot(p["mla_k_gain"]).reshape(depth, 1, HEAD_PAD),
        "k_gain_s": swapped_slot(p["mla_k_gain"]).reshape(depth, 1, HEAD_PAD),
        "cos": cos_t, "sin": sin_s,
        "hg_gain": jnp.tile(p["hg_out_gain"], (1, HG_HEADS)).reshape(depth, 1, HG_W),
        "w_pool": jnp.einsum("gh,lgcd->lgchd", eye, p["pool_w"]).reshape(depth, POOL_W, POOL_W).astype(BF16),
        "pool_scale": p["pool_scale"].reshape(depth, 1, POOL_W),
        "w_out": p["w_out"].astype(BF16),
    }
    ffn = tuple(p[n].astype(BF16) for n in ("ffn1_w_in", "ffn1_w_out", "ffn2_w_in", "ffn2_w_out"))
    return wts, ffn


def kernel(x, c, ctx, c_ctx, w_mod, b_mod, ffn1_w_in, ffn1_w_out, w_in, w_out, hg_lb_logits,
           hg_out_gain, mla_q_a_gain, mla_w_uq, mla_kv_a_gain, mla_w_ukv, mla_q_gain, mla_k_gain,
           pool_w, pool_scale, ffn2_w_in, ffn2_w_out):
    n_batch, seq, d = x.shape
    ctx_len = ctx.shape[1]
    depth = w_mod.shape[0]
    n_lat, n_ctx = n_batch * seq, n_batch * ctx_len
    n_tokens = n_lat + n_ctx
    dims = (n_batch, seq, ctx_len, n_lat, n_tokens, d, depth)
    assert n_batch + 1 <= SUBLANES and seq % FFN_TILE == 0 and n_ctx % FFN_TILE == 0
    wts, (f1_in, f1_out, f2_in, f2_out) = _prepare_weights(dict(
        w_in=w_in, w_out=w_out, hg_lb_logits=hg_lb_logits, hg_out_gain=hg_out_gain,
        mla_q_a_gain=mla_q_a_gain, mla_w_uq=mla_w_uq, mla_kv_a_gain=mla_kv_a_gain, mla_w_ukv=mla_w_ukv,
        mla_q_gain=mla_q_gain, mla_k_gain=mla_k_gain, pool_w=pool_w, pool_scale=pool_scale,
        ffn1_w_in=ffn1_w_in, ffn1_w_out=ffn1_w_out, ffn2_w_in=ffn2_w_in, ffn2_w_out=ffn2_w_out), dims)

    cond = jnp.concatenate([c, c_ctx[None, :], jnp.zeros((SUBLANES - n_batch - 1, d), F32)], axis=0)
    mod = _modulation(cond, w_mod, b_mod).reshape(depth, SUBLANES, N_MOD, d)

    xs = jnp.concatenate([x.reshape(n_lat, d), ctx.reshape(n_ctx, d)], axis=0)
    for l in range(depth):
        last = l == depth - 1
        xs = _ffn(xs, mod, f1_in, f1_out, l, 0, n_tokens, n_lat, seq, n_batch)
        hq, hv, lff, kf, lfb, kb, g, pu, q, k, v = _inproj(xs, mod, wts, l, dims)
        o_f, o_b = _hgrn(hq, hv, lff, kf, lfb, kb, dims)
        att = _attention(q, k, v, dims, with_ctx=not last)
        xs = _outproj(xs, mod, o_f, o_b, g, att, pu, wts, l, dims, with_ctx=not last)
        xs = _ffn(xs, mod, f2_in, f2_out, l, 6, n_lat if last else n_tokens, n_lat, seq, n_batch)
    return xs.reshape(n_batch, seq, d)
```

```python
import functools
import math

import numpy as np
import jax
import jax.numpy as jnp
from jax import lax
from jax.experimental import pallas as pl
from jax.experimental.pallas import tpu as pltpu

F32 = jnp.float32
BF16 = jnp.bfloat16
EPS = 1e-6
F_MIN = 1e-6
N_MOD = 9
GRID_W = 64
ROPE_BASE = 10000.0
HG_HEADS = 4
HG_DK = 64
HG_DV = 64
HG_W = HG_HEADS * HG_DK
HG_CHUNK = 64
HG_SUB = 16
HG_SEQS = 4
MLA_HEADS = 8
MLA_Q_RANK = 384
MLA_KV_RANK = 256
MLA_NOPE = 64
MLA_ROPE = 32
MLA_V = 64
MLA_QK = MLA_NOPE + MLA_ROPE
HEAD_PAD = 128
POOL_WINDOWS = (2, 4, 8, 16)
POOL_CH = 64
POOL_W = len(POOL_WINDOWS) * POOL_CH
POOL_HALO = max(POOL_WINDOWS) // 2
TOKEN_TILE = 256
FFN_TILE = 512
FFN_CHUNK = 256
ATTN_TILE = 2048
ATTN_SUB = 256
LOG2_E = 1.4426950408889634
MASKED_LOG2 = -1e30
VMEM_LIMIT = 56 * 1024 * 1024
SUBLANES = 8
MOD_COLS = 1024

C_Q, C_FF, C_FB, C_I, C_G = (j * HG_W for j in range(5))
C_CQ = 5 * HG_W
C_CKV = C_CQ + MLA_Q_RANK
C_POOL = C_CKV + MLA_KV_RANK
C_KPE = C_POOL + POOL_W
C_KPES = C_KPE + HEAD_PAD
IN_PAD = C_KPES + HEAD_PAD


def _dot(a, b):
    return jnp.dot(a, b, preferred_element_type=F32)


def _dot_nt(a, b):
    return lax.dot_general(a, b, (((1,), (1,)), ((), ())), preferred_element_type=F32)


def _dot_tn(a, b):
    return lax.dot_general(a, b, (((0,), (0,)), ((), ())), preferred_element_type=F32)


def _sigmoid(x):
    return 1.0 / (1.0 + jnp.exp(-x))


def _silu(x):
    return x * _sigmoid(x)


def _split3(x):
    hi = x.astype(BF16)
    r = x - hi.astype(F32)
    mid = r.astype(BF16)
    lo = (r - mid.astype(F32)).astype(BF16)
    return hi, mid, lo


def _exact_dot_left(m_bf16, x):
    hi, mid, lo = _split3(x)
    return _dot(m_bf16, hi) + _dot(m_bf16, mid) + _dot(m_bf16, lo)


def _exact_dot_right(x, m_bf16):
    hi, mid, lo = _split3(x)
    return _dot(hi, m_bf16) + _dot(mid, m_bf16) + _dot(lo, m_bf16)


def _mod_norm(x, shift, scale):
    y = x * lax.rsqrt(jnp.mean(x * x, axis=-1, keepdims=True) + EPS)
    return y * (1.0 + scale) + shift


def _same_group(rows, cols, row_group, col_group):
    r = lax.broadcasted_iota(jnp.int32, (rows, cols), 0) // row_group
    c = lax.broadcasted_iota(jnp.int32, (rows, cols), 1) // col_group
    return r == c


def _mod_kernel(c_ref, w_ref, b_ref, o_ref):
    a = _silu(c_ref[...])
    o_ref[...] = jnp.dot(a, w_ref[...], preferred_element_type=F32,
                         precision=lax.Precision.HIGHEST) + b_ref[...]


def _modulation(cond, w_mod, b_mod):
    depth, d, nd = w_mod.shape
    tn = MOD_COLS
    return pl.pallas_call(
        _mod_kernel,
        grid=(depth, nd // tn),
        in_specs=[
            pl.BlockSpec((SUBLANES, d), lambda l, j: (0, 0)),
            pl.BlockSpec((None, d, tn), lambda l, j: (l, 0, j)),
            pl.BlockSpec((None, 1, tn), lambda l, j: (l, 0, j)),
        ],
        out_specs=pl.BlockSpec((None, SUBLANES, tn), lambda l, j: (l, 0, j)),
        out_shape=jax.ShapeDtypeStruct((depth, SUBLANES, nd), F32),
        compiler_params=pltpu.CompilerParams(
            dimension_semantics=("arbitrary", "arbitrary"), vmem_limit_bytes=VMEM_LIMIT),
        name="adaln_modulation",
    )(cond, w_mod, b_mod.reshape(depth, 1, nd))


def _ffn_kernel(x_ref, mod_ref, win_ref, wout_ref, o_ref, *, row0, d_ff, chunk):
    x = x_ref[...]
    shift = mod_ref[row0:row0 + 1, :]
    scale = mod_ref[row0 + 1:row0 + 2, :]
    gate = mod_ref[row0 + 2:row0 + 3, :]
    h = _mod_norm(x, shift, scale).astype(BF16)
    acc = None
    for j in range(d_ff // chunk):
        g = _dot(h, win_ref[:, j * chunk:(j + 1) * chunk])
        u = _dot(h, win_ref[:, d_ff + j * chunk:d_ff + (j + 1) * chunk])
        a = (_silu(g) * u).astype(BF16)
        p = _dot(a, wout_ref[j * chunk:(j + 1) * chunk, :])
        acc = p if acc is None else acc + p
    o_ref[...] = x + gate * (0.5 * acc)


def _ffn(xs, mod, w_in, w_out, layer, row0, n_tokens, n_lat, seq, n_batch):
    _, d = xs.shape
    d_ff = w_out.shape[1]
    tm = FFN_TILE
    lat_tiles = n_lat // tm

    def mod_map(i):
        return (layer, jnp.where(i < lat_tiles, (i * tm) // seq, n_batch), 0, 0)

    return pl.pallas_call(
        functools.partial(_ffn_kernel, row0=row0, d_ff=d_ff, chunk=FFN_CHUNK),
        grid=(n_tokens // tm,),
        in_specs=[
            pl.BlockSpec((tm, d), lambda i: (i, 0)),
            pl.BlockSpec((None, None, N_MOD, d), mod_map),
            pl.BlockSpec((None, d, 2 * d_ff), lambda i: (layer, 0, 0)),
            pl.BlockSpec((None, d_ff, d), lambda i: (layer, 0, 0)),
        ],
        out_specs=pl.BlockSpec((tm, d), lambda i: (i, 0)),
        out_shape=jax.ShapeDtypeStruct((n_tokens, d), F32),
        compiler_params=pltpu.CompilerParams(
            dimension_semantics=("arbitrary",), vmem_limit_bytes=VMEM_LIMIT),
        name="swiglu_half_step",
    )(xs, mod, w_in, w_out)


def _pair_norm_rope(x, xs, gc, gs, slot_ones):
    ss = _dot((x * x).astype(BF16), slot_ones)
    return (x * gc + xs * gs) * lax.rsqrt(ss * (1.0 / MLA_QK) + EPS)


def _inproj_kernel(x_ref, mod_ref, w_ref, lbl_ref, qag_ref, wuq_ref, kvag_ref, wuk_ref, wuv_ref,
                   qg_ref, qgs_ref, kg_ref, kgs_ref, cos_ref, sin_ref,
                   hq_ref, hv_ref, lff_ref, kf_ref, lfb_ref, kb_ref, g_ref, pu_ref,
                   q_ref, k_ref, v_ref, *, layer):
    x = x_ref[...]
    h = _mod_norm(x, mod_ref[3:4, :], mod_ref[4:5, :]).astype(BF16)
    y = _dot(h, w_ref[...])

    logits = lbl_ref[...]
    e = jnp.exp(logits - jnp.max(logits, axis=0, keepdims=True))
    p = e / jnp.sum(e, axis=0, keepdims=True)
    lb = jnp.sum(p[:layer + 1], axis=0) - p[0]

    hq_ref[...] = _silu(y[:, C_Q:C_Q + HG_W])
    hv_ref[...] = y[:, C_I:C_I + HG_W]
    g_ref[...] = _silu(y[:, C_G:C_G + HG_W])
    for col, lrow, lf_ref, kk_ref in ((C_FF, 0, lff_ref, kf_ref), (C_FB, 1, lfb_ref, kb_ref)):
        lbv = lb[lrow:lrow + 1, :]
        f = lbv + (1.0 - lbv) * _sigmoid(y[:, col:col + HG_W])
        lf_ref[...] = jnp.log(jnp.clip(f, F_MIN, 1.0))
        kk_ref[...] = 1.0 - f
    pu_ref[...] = y[:, C_POOL:C_POOL + POOL_W]

    wide = MLA_HEADS * HEAD_PAD
    cq = y[:, C_CQ:C_CQ + MLA_Q_RANK]
    cqn = cq * lax.rsqrt(jnp.mean(cq * cq, axis=-1, keepdims=True) + EPS) * qag_ref[...]
    qf = _dot(cqn.astype(BF16), wuq_ref[...])
    ckv = y[:, C_CKV:C_CKV + MLA_KV_RANK]
    ckvn = (ckv * lax.rsqrt(jnp.mean(ckv * ckv, axis=-1, keepdims=True) + EPS) * kvag_ref[...]).astype(BF16)
    kn = _dot(ckvn, wuk_ref[...])
    v_ref[...] = _dot(ckvn, wuv_ref[...]).astype(BF16)
    kpe = y[:, C_KPE:C_KPE + HEAD_PAD]
    kpe2 = jnp.concatenate([kpe, kpe], axis=1)
    kpes = y[:, C_KPES:C_KPES + HEAD_PAD]
    kpes2 = jnp.concatenate([kpes, kpes], axis=1)

    def pair_tables(g_ref, gs_ref, scale):
        gc = g_ref[...] * scale * cos_ref[...]
        gs = gs_ref[...] * scale * sin_ref[...]
        return jnp.concatenate([gc, gc], axis=1), jnp.concatenate([gs, gs], axis=1)

    q_gc, q_gs = pair_tables(qg_ref, qgs_ref, MLA_QK ** -0.5 * LOG2_E)
    k_gc, k_gs = pair_tables(kg_ref, kgs_ref, 1.0)
    slot_ones = _same_group(2 * HEAD_PAD, 2 * HEAD_PAD, HEAD_PAD, HEAD_PAD).astype(BF16)
    for pr in range(MLA_HEADS // 2):
        sl = slice(pr * 2 * HEAD_PAD, (pr + 1) * 2 * HEAD_PAD)
        sls = slice(wide + pr * 2 * HEAD_PAD, wide + (pr + 1) * 2 * HEAD_PAD)
        q_ref[:, sl] = _pair_norm_rope(qf[:, sl], qf[:, sls], q_gc, q_gs, slot_ones).astype(BF16)
        kh = _pair_norm_rope(kn[:, sl] + kpe2, kpes2, k_gc, k_gs, slot_ones)
        k_ref[sl, :] = kh.T.astype(BF16)


def _inproj(xs, mod, wts, layer, dims):
    n_batch, seq, ctx, n_lat, n_tokens, d, depth = dims
    tm = TOKEN_TILE
    lat_tiles, per_b = n_lat // tm, seq // tm
    s_all = seq + ctx

    def b_of(i):
        return jnp.where(i < lat_tiles, i // per_b, i - lat_tiles)

    def r_of(i):
        return jnp.where(i < lat_tiles, i % per_b, per_b)

    def const(*shape):
        return pl.BlockSpec(shape, lambda i: (0,) * len(shape))

    def per_layer(*shape):
        return pl.BlockSpec((None,) + shape, lambda i: (layer,) + (0,) * len(shape))

    def seq_out(width):
        return pl.BlockSpec((None, tm, width), lambda i: (b_of(i), r_of(i), 0))

    rope_spec = pl.BlockSpec((tm, HEAD_PAD), lambda i: (r_of(i), 0))
    f32_out = jax.ShapeDtypeStruct((n_batch, s_all, HG_W), F32)
    wide = MLA_HEADS * HEAD_PAD
    return pl.pallas_call(
        functools.partial(_inproj_kernel, layer=layer),
        grid=(n_tokens // tm,),
        in_specs=[
            pl.BlockSpec((tm, d), lambda i: (i, 0)),
            pl.BlockSpec((None, None, N_MOD, d), lambda i: (layer, jnp.where(i < lat_tiles, i // per_b, n_batch), 0, 0)),
            per_layer(d, IN_PAD),
            const(depth, 2, HG_W),
            per_layer(1, MLA_Q_RANK), per_layer(MLA_Q_RANK, 2 * wide),
            per_layer(1, MLA_KV_RANK), per_layer(MLA_KV_RANK, wide), per_layer(MLA_KV_RANK, MLA_HEADS * MLA_V),
            per_layer(1, HEAD_PAD), per_layer(1, HEAD_PAD), per_layer(1, HEAD_PAD), per_layer(1, HEAD_PAD),
            rope_spec, rope_spec,
        ],
        out_specs=[seq_out(HG_W)] * 8 + [
            seq_out(wide), pl.BlockSpec((None, wide, tm), lambda i: (b_of(i), 0, r_of(i))),
            seq_out(MLA_HEADS * MLA_V)],
        out_shape=[f32_out] * 8 + [
            jax.ShapeDtypeStruct((n_batch, s_all, wide), BF16),
            jax.ShapeDtypeStruct((n_batch, wide, s_all), BF16),
            jax.ShapeDtypeStruct((n_batch, s_all, MLA_HEADS * MLA_V), BF16)],
        compiler_params=pltpu.CompilerParams(
            dimension_semantics=("arbitrary",), vmem_limit_bytes=VMEM_LIMIT),
        name="mixer_in_projection",
    )(xs, mod, wts["w_in"], wts["lb_logits"], wts["q_a_gain"], wts["w_uq"], wts["kv_a_gain"],
      wts["w_uk"], wts["w_uv"], wts["q_gain"], wts["q_gain_s"], wts["k_gain"], wts["k_gain_s"],
      wts["cos"], wts["sin"])


def _hg_chunk(q, k, v, lf, st, rev):
    c, sub = HG_CHUNK, HG_SUB
    n_sub = c // sub
    ri = lax.broadcasted_iota(jnp.int32, (c, c), 0)
    ci = lax.broadcasted_iota(jnp.int32, (c, c), 1)
    tri = ((ci >= ri) if rev else (ci <= ri)).astype(BF16)
    b = _exact_dot_left(tri, lf) * LOG2_E
    b_tot = b[0:1, :] if rev else b[c - 1:c, :]
    head_sq = _same_group(HG_W, HG_W, HG_DV, HG_DK)
    head_sub = _same_group(n_sub * sub, HG_W, sub, HG_DK)
    block_ones = head_sq.astype(BF16)

    o = _dot_nt((q * jnp.exp2(b)).astype(BF16), st.astype(BF16))
    kd = (k * jnp.exp2(b_tot - b)).astype(BF16)
    upd = _dot_tn(v.astype(BF16), kd)
    st_new = st * jnp.exp2(b_tot) + jnp.where(head_sq, upd, 0.0)

    def rows(a, i):
        return a[i * sub:(i + 1) * sub, :]

    parts = [[rows(o, i)] for i in range(n_sub)]
    for j in (range(1, n_sub) if rev else range(n_sub - 1)):
        kj, bj, vj = rows(k, j), rows(b, j), rows(v, j)
        ref = bj[0:1, :] if rev else bj[sub - 1:sub, :]
        kh = kj * jnp.exp2(ref - bj)
        kh4 = jnp.where(head_sub, jnp.concatenate([kh] * HG_HEADS, axis=0), 0.0).astype(BF16)
        v4 = jnp.where(head_sub, jnp.concatenate([vj] * HG_HEADS, axis=0), 0.0).astype(BF16)
        lo, hi = (0, j * sub) if rev else ((j + 1) * sub, c)
        qt = (q[lo:hi, :] * jnp.exp2(b[lo:hi, :] - ref)).astype(BF16)
        att = _dot_nt(qt, kh4)
        oa = _dot(att.astype(BF16), v4)
        for i in range(lo // sub, hi // sub):
            parts[i].append(oa[i * sub - lo:(i + 1) * sub - lo, :])
    g = sub // 2
    cidx = lax.broadcasted_iota(jnp.int32, (sub, HG_W), 0)
    full_keys = range(g, sub) if rev else range(g)
    half_keys = range(g) if rev else range(g, sub)
    hrows = slice(0, g) if rev else slice(g, sub)
    cidx_half = lax.broadcasted_iota(jnp.int32, (g, HG_W), 0) + hrows.start

    def pair_terms(qq, bb, cc, ks, bs, s):
        keep = (cc <= s) if rev else (cc >= s)
        return qq * ks * jnp.exp2(jnp.where(keep, bb - bs, MASKED_LOG2))

    pieces = []
    for i in range(n_sub):
        qi, ki, bi = rows(q, i), rows(k, i), rows(b, i)
        for s in full_keys:
            pieces.append(pair_terms(qi, bi, cidx, ki[s:s + 1, :], bi[s:s + 1, :], s).astype(BF16))
        halves = [pair_terms(qi[hrows, :], bi[hrows, :], cidx_half, ki[s:s + 1, :], bi[s:s + 1, :], s)
                  for s in half_keys]
        for h0, h1 in zip(halves[0::2], halves[1::2]):
            pieces.append(jnp.concatenate([h0, h1], axis=0).astype(BF16))
    r = _dot(jnp.concatenate(pieces, axis=0), block_ones)
    off = 0
    for i in range(n_sub):
        vi = rows(v, i)
        acc_full = None
        for s in full_keys:
            t = r[off:off + sub, :] * vi[s:s + 1, :]
            acc_full = t if acc_full is None else acc_full + t
            off += sub
        acc_half = None
        for s0, s1 in zip(half_keys[0::2], half_keys[1::2]):
            t = r[off:off + g, :] * vi[s0:s0 + 1, :] + r[off + g:off + sub, :] * vi[s1:s1 + 1, :]
            acc_half = t if acc_half is None else acc_half + t
            off += sub
        lo_half, hi_half = acc_full[:g, :], acc_full[g:, :]
        if rev:
            lo_half = lo_half + acc_half
        else:
            hi_half = hi_half + acc_half
        parts[i].append(jnp.concatenate([lo_half, hi_half], axis=0))
    outs = []
    for plist in parts:
        tot = plist[0]
        for t in plist[1:]:
            tot = tot + t
        outs.append(tot)
    return jnp.concatenate(outs, axis=0), st_new


def _hgrn_kernel(qf_ref, vf_ref, lff_ref, kf_ref, qb_ref, vb_ref, lfb_ref, kb_ref,
                 of_ref, ob_ref, sf_ref, sb_ref, *, tb):
    @pl.when(pl.program_id(1) == 0)
    def _():
        sf_ref[...] = jnp.zeros_like(sf_ref)
        sb_ref[...] = jnp.zeros_like(sb_ref)

    n_chunks = tb // HG_CHUNK
    n_seq = qf_ref.shape[0]
    sf = [sf_ref[j] for j in range(n_seq)]
    sb = [sb_ref[j] for j in range(n_seq)]
    for ci in range(n_chunks):
        sl = slice(ci * HG_CHUNK, (ci + 1) * HG_CHUNK)
        sr = slice((n_chunks - 1 - ci) * HG_CHUNK, (n_chunks - ci) * HG_CHUNK)
        for j in range(n_seq):
            of_ref[j, sl, :], sf[j] = _hg_chunk(qf_ref[j, sl, :], kf_ref[j, sl, :], vf_ref[j, sl, :],
                                                 lff_ref[j, sl, :], sf[j], False)
            ob_ref[j, sr, :], sb[j] = _hg_chunk(qb_ref[j, sr, :], kb_ref[j, sr, :], vb_ref[j, sr, :],
                                                 lfb_ref[j, sr, :], sb[j], True)
    for j in range(n_seq):
        sf_ref[j] = sf[j]
        sb_ref[j] = sb[j]


def _hgrn(hq, hv, lff, kf, lfb, kb, dims):
    n_batch, seq, ctx = dims[0], dims[1], dims[2]
    tb = TOKEN_TILE
    assert ctx == tb and seq % tb == 0
    per_b = seq // tb
    n_seq = math.gcd(n_batch, HG_SEQS)
    fwd = pl.BlockSpec((n_seq, tb, HG_W), lambda b, n: (b, jnp.where(n == 0, per_b, n - 1), 0))
    bwd = pl.BlockSpec((n_seq, tb, HG_W), lambda b, n: (b, jnp.where(n == 0, per_b, per_b - n), 0))
    shape = jax.ShapeDtypeStruct(hq.shape, F32)
    return pl.pallas_call(
        functools.partial(_hgrn_kernel, tb=tb),
        grid=(n_batch // n_seq, per_b + 1),
        in_specs=[fwd, fwd, fwd, fwd, bwd, bwd, bwd, bwd],
        out_specs=[fwd, bwd],
        out_shape=[shape, shape],
        scratch_shapes=[pltpu.VMEM((n_seq, HG_W, HG_W), F32), pltpu.VMEM((n_seq, HG_W, HG_W), F32)],
        compiler_params=pltpu.CompilerParams(
            dimension_semantics=("arbitrary", "arbitrary"), vmem_limit_bytes=VMEM_LIMIT),
        name="hgrn2_bidirectional_scan",
    )(hq, hv, lff, kf, hq, hv, lfb, kb)


def _attn_kernel(q_ref, kt_ref, v_ref, *rest):
    o_ref = rest[-1]
    tq = q_ref.shape[0]
    sub = min(ATTN_SUB, tq)
    lane = lax.broadcasted_iota(jnp.int32, (sub, 2 * MLA_V), 1)
    for t in range(tq // sub):
        rows = slice(t * sub, (t + 1) * sub)
        outs = []
        for hh in range(2):
            hs = slice(hh * HEAD_PAD, (hh + 1) * HEAD_PAD)
            s = _dot(q_ref[rows, hs], kt_ref[hs, :])
            p = jnp.exp2(s - jnp.max(s, axis=-1, keepdims=True))
            l = jnp.sum(p, axis=-1, keepdims=True)
            outs.append(_dot(p.astype(BF16), v_ref[...]) / l)
        o_ref[rows, :] = jnp.where(lane < MLA_V, outs[0], outs[1]).astype(o_ref.dtype)


def _attention(q, kt, v, dims, with_ctx):
    n_batch, seq, ctx = dims[0], dims[1], dims[2]
    s_all = seq + ctx
    tq = min(ATTN_TILE, seq)
    assert seq % tq == 0 and seq % ctx == 0
    pairs = MLA_HEADS // 2
    out_shape = jax.ShapeDtypeStruct((n_batch, s_all, MLA_HEADS * MLA_V), BF16)
    params = pltpu.CompilerParams(
        dimension_semantics=("arbitrary", "arbitrary", "arbitrary"), vmem_limit_bytes=VMEM_LIMIT)
    att = pl.pallas_call(
        _attn_kernel,
        grid=(n_batch, pairs, seq // tq),
        in_specs=[
            pl.BlockSpec((None, tq, 2 * HEAD_PAD), lambda b, p, i: (b, i, p)),
            pl.BlockSpec((None, 2 * HEAD_PAD, s_all), lambda b, p, i: (b, p, 0)),
            pl.BlockSpec((None, s_all, 2 * MLA_V), lambda b, p, i: (b, 0, p)),
        ],
        out_specs=pl.BlockSpec((None, tq, 2 * MLA_V), lambda b, p, i: (b, i, p)),
        out_shape=out_shape,
        compiler_params=params,
        name="latent_attention",
    )(q, kt, v)
    if not with_ctx:
        return att
    cb = seq // ctx
    return pl.pallas_call(
        _attn_kernel,
        grid=(n_batch, pairs, 1),
        in_specs=[
            pl.BlockSpec((None, ctx, 2 * HEAD_PAD), lambda b, p, i: (b, cb, p)),
            pl.BlockSpec((None, 2 * HEAD_PAD, ctx), lambda b, p, i: (b, p, cb)),
            pl.BlockSpec((None, ctx, 2 * MLA_V), lambda b, p, i: (b, cb, p)),
            pl.BlockSpec(memory_space=pl.ANY),
        ],
        out_specs=pl.BlockSpec((None, ctx, 2 * MLA_V), lambda b, p, i: (b, cb, p)),
        out_shape=out_shape,
        input_output_aliases={3: 0},
        compiler_params=params,
        name="context_attention",
    )(q, kt, v, att)


def _outproj_kernel(x_ref, mod_ref, of_ref, ob_ref, g_ref, att_ref, pu_ref, pprev_ref, pnext_ref,
                    hgain_ref, wp_ref, pscale_ref, wout_ref, o_ref, *, lat_tiles, per_b, seq, ctx):
    i = pl.program_id(0)
    tm = x_ref.shape[0]
    is_lat = i < lat_tiles
    r = jnp.where(is_lat, i % per_b, 0)
    seq_len = jnp.where(is_lat, seq, ctx)
    seq_tiles = jnp.where(is_lat, per_b, ctx // tm)

    o = of_ref[...] + ob_ref[...]
    block_ones = _same_group(HG_W, HG_W, HG_DV, HG_DV).astype(BF16)
    ms = _exact_dot_right(o * o, block_ones) * (1.0 / HG_DV)
    hg = o * lax.rsqrt(ms + EPS) * hgain_ref[...] * g_ref[...]

    h8 = POOL_HALO
    prev = jnp.where(r > 0, pprev_ref[...], 0.0)
    nxt = jnp.where(r < seq_tiles - 1, pnext_ref[...], 0.0)
    u = pu_ref[...]
    ext = jnp.concatenate([prev, u, nxt], axis=0)
    n_ext = tm + 2 * h8

    def shifted(a, delta):
        return pltpu.roll(a, (-delta) % n_ext, 0)

    s2 = ext + shifted(ext, -1)
    s4 = shifted(s2, -1) + shifted(s2, 1)
    s8 = shifted(s4, -2) + shifted(s4, 2)
    s16 = shifted(s8, -4) + shifted(s8, 4)
    pos = r * tm + lax.broadcasted_iota(jnp.int32, (tm, POOL_W), 0)
    lane_g = lax.broadcasted_iota(jnp.int32, (tm, POOL_W), 1) // POOL_CH
    def by_group(values):
        out = values[-1]
        for g in range(len(values) - 2, -1, -1):
            out = jnp.where(lane_g == g, values[g], out)
        return out

    back = by_group([w // 2 for w in POOL_WINDOWS])
    ahead = by_group([w - 1 - w // 2 for w in POOL_WINDOWS])
    cnt = jnp.minimum(pos + ahead, seq_len - 1) - jnp.maximum(pos - back, 0) + 1
    sums = by_group([a[h8:h8 + tm, :] for a in (s2, s4, s8, s16)])
    pooled = sums / cnt.astype(F32) - u
    pool = _dot(pooled.astype(BF16), wp_ref[...]) * pscale_ref[...]

    mix = jnp.concatenate([hg.astype(BF16), att_ref[...], pool.astype(BF16)], axis=1)
    o_ref[...] = x_ref[...] + mod_ref[5:6, :] * _dot(mix, wout_ref[...])


def _outproj(xs, mod, o_f, o_b, g, att, pu, wts, layer, dims, with_ctx):
    n_batch, seq, ctx, n_lat, n_tokens, d, depth = dims
    tm = TOKEN_TILE
    lat_tiles, per_b = n_lat // tm, seq // tm
    n_out = n_tokens if with_ctx else n_lat
    s_all = seq + ctx
    hb = tm // POOL_HALO

    def b_of(i):
        return jnp.where(i < lat_tiles, i // per_b, i - lat_tiles)

    def r_of(i):
        return jnp.where(i < lat_tiles, i % per_b, per_b)

    def seq_in(width):
        return pl.BlockSpec((None, tm, width), lambda i: (b_of(i), r_of(i), 0))

    def per_layer(*shape):
        return pl.BlockSpec((None,) + shape, lambda i: (layer,) + (0,) * len(shape))

    last_halo = s_all // POOL_HALO - 1
    prev_spec = pl.BlockSpec((None, POOL_HALO, POOL_W),
                             lambda i: (b_of(i), jnp.maximum(r_of(i) * hb - 1, 0), 0))
    next_spec = pl.BlockSpec((None, POOL_HALO, POOL_W),
                             lambda i: (b_of(i), jnp.minimum((r_of(i) + 1) * hb, last_halo), 0))
    return pl.pallas_call(
        functools.partial(_outproj_kernel, lat_tiles=lat_tiles, per_b=per_b, seq=seq, ctx=ctx),
        grid=(n_out // tm,),
        in_specs=[
            pl.BlockSpec((tm, d), lambda i: (i, 0)),
            pl.BlockSpec((None, None, N_MOD, d), lambda i: (layer, jnp.where(i < lat_tiles, i // per_b, n_batch), 0, 0)),
            seq_in(HG_W), seq_in(HG_W), seq_in(HG_W), seq_in(MLA_HEADS * MLA_V), seq_in(POOL_W),
            prev_spec, next_spec,
            per_layer(1, HG_W), per_layer(POOL_W, POOL_W), per_layer(1, POOL_W), per_layer(d, d),
        ],
        out_specs=pl.BlockSpec((tm, d), lambda i: (i, 0)),
        out_shape=jax.ShapeDtypeStruct((n_out, d), F32),
        compiler_params=pltpu.CompilerParams(
            dimension_semantics=("arbitrary",), vmem_limit_bytes=VMEM_LIMIT),
        name="mixer_out_projection",
    )(xs, mod, o_f, o_b, g, att, pu, pu, pu, wts["hg_gain"], wts["w_pool"], wts["pool_scale"], wts["w_out"])


def _rope_tables(seq, ctx):
    rows = seq // GRID_W
    row = np.repeat(np.arange(rows), GRID_W).astype(np.float32)
    col = np.tile(np.arange(GRID_W), rows).astype(np.float32)
    n_freq = MLA_ROPE // 4
    inv_freq = jnp.asarray(ROPE_BASE, F32) ** (-jnp.arange(n_freq, dtype=F32) / n_freq)
    ang = jnp.concatenate([jnp.asarray(row)[:, None] * inv_freq, jnp.asarray(col)[:, None] * inv_freq], axis=-1)
    cos, sin = jnp.cos(ang), jnp.sin(ang)
    half = MLA_ROPE // 2
    zeros = lambda w: jnp.zeros((seq, w), F32)
    cos_t = jnp.concatenate([jnp.ones((seq, MLA_NOPE), F32), cos, cos, zeros(HEAD_PAD - MLA_QK)], axis=1)
    sin_s = jnp.concatenate([zeros(MLA_NOPE), -sin, sin, zeros(HEAD_PAD - MLA_QK)], axis=1)
    ident = jnp.concatenate([jnp.ones((ctx, MLA_QK), F32), jnp.zeros((ctx, HEAD_PAD - MLA_QK), F32)], axis=1)
    return (jnp.concatenate([cos_t, ident], axis=0),
            jnp.concatenate([sin_s, jnp.zeros((ctx, HEAD_PAD), F32)], axis=0))


def _prepare_weights(p, dims):
    seq, ctx_len, d, depth = dims[1], dims[2], dims[5], dims[6]
    pad_cols = lambda a, n: jnp.pad(a, [(0, 0)] * (a.ndim - 1) + [(0, n)])
    w_in = p["w_in"]
    half = MLA_ROPE // 2

    def slot(a):
        return pad_cols(a, HEAD_PAD - MLA_QK)

    def swapped_slot(a):
        return jnp.concatenate([jnp.zeros_like(a[..., :MLA_NOPE]), a[..., MLA_NOPE + half:],
                                a[..., MLA_NOPE:MLA_NOPE + half], jnp.zeros_like(a[..., :HEAD_PAD - MLA_QK])], axis=-1)

    kpe_cols = jnp.concatenate([jnp.zeros((depth, d, MLA_NOPE), F32), w_in[:, :, C_POOL:C_POOL + MLA_ROPE]], axis=-1)
    w_in_p = jnp.concatenate([
        w_in[:, :, :C_POOL], w_in[:, :, C_POOL + MLA_ROPE:], slot(kpe_cols), swapped_slot(kpe_cols)],
        axis=-1).astype(BF16)
    w_uq4 = p["mla_w_uq"].reshape(depth, MLA_Q_RANK, MLA_HEADS, MLA_QK)
    wide = MLA_HEADS * HEAD_PAD
    w_uq = jnp.concatenate([slot(w_uq4).reshape(depth, MLA_Q_RANK, wide),
                            swapped_slot(w_uq4).reshape(depth, MLA_Q_RANK, wide)], axis=-1)
    w_ukv = p["mla_w_ukv"].reshape(depth, MLA_KV_RANK, MLA_HEADS, MLA_NOPE + MLA_V)
    cos_t, sin_s = _rope_tables(seq, ctx_len)
    eye = jnp.eye(len(POOL_WINDOWS), dtype=F32)
    wts = {
        "w_in": w_in_p,
        "lb_logits": p["hg_lb_logits"].astype(F32),
        "q_a_gain": p["mla_q_a_gain"].reshape(depth, 1, MLA_Q_RANK),
        "w_uq": w_uq.astype(BF16),
        "kv_a_gain": p["mla_kv_a_gain"].reshape(depth, 1, MLA_KV_RANK),
        "w_uk": pad_cols(w_ukv[..., :MLA_NOPE], HEAD_PAD - MLA_NOPE).reshape(
            depth, MLA_KV_RANK, MLA_HEADS * HEAD_PAD).astype(BF16),
        "w_uv": w_ukv[..., MLA_NOPE:].reshape(depth, MLA_KV_RANK, MLA_HEADS * MLA_V).astype(BF16),
        "q_gain": slot(p["mla_q_gain"]).reshape(depth, 1, HEAD_PAD),
        "q_gain_s": swapped_slot(p["mla_q_gain"]).reshape(depth, 1, HEAD_PAD),
        "k_gain": slot(p["mla_k_gain"]).reshape(depth, 1, HEAD_PAD),
        "k_gain_s": swapped_slot(p["mla_k_gain"]).reshape(depth, 1, HEAD_PAD),
        "cos": cos_t, "sin": sin_s,
        "hg_gain": jnp.tile(p["hg_out_gain"], (1, HG_HEADS)).reshape(depth, 1, HG_W),
        "w_pool": jnp.einsum("gh,lgcd->lgchd", eye, p["pool_w"]).reshape(depth, POOL_W, POOL_W).astype(BF16),
        "pool_scale": p["pool_scale"].reshape(depth, 1, POOL_W),
        "w_out": p["w_out"].astype(BF16),
    }
    ffn = tuple(p[n].astype(BF16) for n in ("ffn1_w_in", "ffn1_w_out", "ffn2_w_in", "ffn2_w_out"))
    return wts, ffn


def kernel(x, c, ctx, c_ctx, w_mod, b_mod, ffn1_w_in, ffn1_w_out, w_in, w_out, hg_lb_logits,
           hg_out_gain, mla_q_a_gain, mla_w_uq, mla_kv_a_gain, mla_w_ukv, mla_q_gain, mla_k_gain,
           pool_w, pool_scale, ffn2_w_in, ffn2_w_out):
    n_batch, seq, d = x.shape
    ctx_len = ctx.shape[1]
    depth = w_mod.shape[0]
    n_lat, n_ctx = n_batch * seq, n_batch * ctx_len
    n_tokens = n_lat + n_ctx
    dims = (n_batch, seq, ctx_len, n_lat, n_tokens, d, depth)
    assert n_batch + 1 <= SUBLANES and seq % FFN_TILE == 0 and n_ctx % FFN_TILE == 0
    wts, (f1_in, f1_out, f2_in, f2_out) = _prepare_weights(dict(
        w_in=w_in, w_out=w_out, hg_lb_logits=hg_lb_logits, hg_out_gain=hg_out_gain,
        mla_q_a_gain=mla_q_a_gain, mla_w_uq=mla_w_uq, mla_kv_a_gain=mla_kv_a_gain, mla_w_ukv=mla_w_ukv,
        mla_q_gain=mla_q_gain, mla_k_gain=mla_k_gain, pool_w=pool_w, pool_scale=pool_scale,
        ffn1_w_in=ffn1_w_in, ffn1_w_out=ffn1_w_out, ffn2_w_in=ffn2_w_in, ffn2_w_out=ffn2_w_out), dims)

    cond = jnp.concatenate([c, c_ctx[None, :], jnp.zeros((SUBLANES - n_batch - 1, d), F32)], axis=0)
    mod = _modulation(cond, w_mod, b_mod).reshape(depth, SUBLANES, N_MOD, d)

    xs = jnp.concatenate([x.reshape(n_lat, d), ctx.reshape(n_ctx, d)], axis=0)
    for l in range(depth):
        last = l == depth - 1
        xs = _ffn(xs, mod, f1_in, f1_out, l, 0, n_tokens, n_lat, seq, n_batch)
        hq, hv, lff, kf, lfb, kb, g, pu, q, k, v = _inproj(xs, mod, wts, l, dims)
        o_f, o_b = _hgrn(hq, hv, lff, kf, lfb, kb, dims)
        att = _attention(q, k, v, dims, with_ctx=not last)
        xs = _outproj(xs, mod, o_f, o_b, g, att, pu, wts, l, dims, with_ctx=not last)
        xs = _ffn(xs, mod, f2_in, f2_out, l, 6, n_lat if last else n_tokens, n_lat, seq, n_batch)
    return xs.reshape(n_batch, seq, d)
```

```python
import functools
import math

import numpy as np
import jax
import jax.numpy as jnp
from jax import lax
from jax.experimental import pallas as pl
from jax.experimental.pallas import tpu as pltpu

F32 = jnp.float32
BF16 = jnp.bfloat16
EPS = 1e-6
F_MIN = 1e-6
N_MOD = 9
GRID_W = 64
ROPE_BASE = 10000.0
HG_HEADS = 4
HG_DK = 64
HG_DV = 64
HG_W = HG_HEADS * HG_DK
HG_CHUNK = 64
HG_SUB = 16
HG_SEQS = 4
MLA_HEADS = 8
MLA_Q_RANK = 384
MLA_KV_RANK = 256
MLA_NOPE = 64
MLA_ROPE = 32
MLA_V = 64
MLA_QK = MLA_NOPE + MLA_ROPE
HEAD_PAD = 128
POOL_WINDOWS = (2, 4, 8, 16)
POOL_CH = 64
POOL_W = len(POOL_WINDOWS) * POOL_CH
POOL_HALO = max(POOL_WINDOWS) // 2
TOKEN_TILE = 256
FFN_TILE = 512
FFN_CHUNK = 256
ATTN_TILE = 1024
ATTN_SUB = 256
LOG2_E = 1.4426950408889634
MASKED_LOG2 = -1e30
VMEM_LIMIT = 56 * 1024 * 1024
SUBLANES = 8
MOD_COLS = 1024

C_Q, C_FF, C_FB, C_I, C_G = (j * HG_W for j in range(5))
C_CQ = 5 * HG_W
C_CKV = C_CQ + MLA_Q_RANK
C_POOL = C_CKV + MLA_KV_RANK
C_KPE = C_POOL + POOL_W
C_KPES = C_KPE + HEAD_PAD
IN_PAD = C_KPES + HEAD_PAD


def _dot(a, b):
    return jnp.dot(a, b, preferred_element_type=F32)


def _dot_nt(a, b):
    return lax.dot_general(a, b, (((1,), (1,)), ((), ())), preferred_element_type=F32)


def _dot_tn(a, b):
    return lax.dot_general(a, b, (((0,), (0,)), ((), ())), preferred_element_type=F32)


def _sigmoid(x):
    return 1.0 / (1.0 + jnp.exp(-x))


def _silu(x):
    return x * _sigmoid(x)


def _split3(x):
    hi = x.astype(BF16)
    r = x - hi.astype(F32)
    mid = r.astype(BF16)
    lo = (r - mid.astype(F32)).astype(BF16)
    return hi, mid, lo


def _exact_dot_left(m_bf16, x):
    hi, mid, lo = _split3(x)
    return _dot(m_bf16, hi) + _dot(m_bf16, mid) + _dot(m_bf16, lo)


def _exact_dot_right(x, m_bf16):
    hi, mid, lo = _split3(x)
    return _dot(hi, m_bf16) + _dot(mid, m_bf16) + _dot(lo, m_bf16)


def _mod_norm(x, shift, scale):
    y = x * lax.rsqrt(jnp.mean(x * x, axis=-1, keepdims=True) + EPS)
    return y * (1.0 + scale) + shift


def _same_group(rows, cols, row_group, col_group):
    r = lax.broadcasted_iota(jnp.int32, (rows, cols), 0) // row_group
    c = lax.broadcasted_iota(jnp.int32, (rows, cols), 1) // col_group
    return r == c


def _mod_kernel(c_ref, w_ref, b_ref, o_ref):
    a = _silu(c_ref[...])
    o_ref[...] = jnp.dot(a, w_ref[...], preferred_element_type=F32,
                         precision=lax.Precision.HIGHEST) + b_ref[...]


def _modulation(cond, w_mod, b_mod):
    depth, d, nd = w_mod.shape
    tn = MOD_COLS
    return pl.pallas_call(
        _mod_kernel,
        grid=(depth, nd // tn),
        in_specs=[
            pl.BlockSpec((SUBLANES, d), lambda l, j: (0, 0)),
            pl.BlockSpec((None, d, tn), lambda l, j: (l, 0, j)),
            pl.BlockSpec((None, 1, tn), lambda l, j: (l, 0, j)),
        ],
        out_specs=pl.BlockSpec((None, SUBLANES, tn), lambda l, j: (l, 0, j)),
        out_shape=jax.ShapeDtypeStruct((depth, SUBLANES, nd), F32),
        compiler_params=pltpu.CompilerParams(
            dimension_semantics=("arbitrary", "arbitrary"), vmem_limit_bytes=VMEM_LIMIT),
        name="adaln_modulation",
    )(cond, w_mod, b_mod.reshape(depth, 1, nd))


def _ffn_kernel(x_ref, mod_ref, win_ref, wout_ref, o_ref, *, row0, d_ff, chunk):
    x = x_ref[...]
    shift = mod_ref[row0:row0 + 1, :]
    scale = mod_ref[row0 + 1:row0 + 2, :]
    gate = mod_ref[row0 + 2:row0 + 3, :]
    h = _mod_norm(x, shift, scale).astype(BF16)
    acc = None
    for j in range(d_ff // chunk):
        g = _dot(h, win_ref[:, j * chunk:(j + 1) * chunk])
        u = _dot(h, win_ref[:, d_ff + j * chunk:d_ff + (j + 1) * chunk])
        a = (_silu(g) * u).astype(BF16)
        p = _dot(a, wout_ref[j * chunk:(j + 1) * chunk, :])
        acc = p if acc is None else acc + p
    o_ref[...] = x + gate * (0.5 * acc)


def _ffn(xs, mod, w_in, w_out, layer, row0, n_tokens, n_lat, seq, n_batch):
    _, d = xs.shape
    d_ff = w_out.shape[1]
    tm = FFN_TILE
    lat_tiles = n_lat // tm

    def mod_map(i):
        return (layer, jnp.where(i < lat_tiles, (i * tm) // seq, n_batch), 0, 0)

    return pl.pallas_call(
        functools.partial(_ffn_kernel, row0=row0, d_ff=d_ff, chunk=FFN_CHUNK),
        grid=(n_tokens // tm,),
        in_specs=[
            pl.BlockSpec((tm, d), lambda i: (i, 0)),
            pl.BlockSpec((None, None, N_MOD, d), mod_map),
            pl.BlockSpec((None, d, 2 * d_ff), lambda i: (layer, 0, 0)),
            pl.BlockSpec((None, d_ff, d), lambda i: (layer, 0, 0)),
        ],
        out_specs=pl.BlockSpec((tm, d), lambda i: (i, 0)),
        out_shape=jax.ShapeDtypeStruct((n_tokens, d), F32),
        compiler_params=pltpu.CompilerParams(
            dimension_semantics=("arbitrary",), vmem_limit_bytes=VMEM_LIMIT),
        name="swiglu_half_step",
    )(xs, mod, w_in, w_out)


def _pair_norm_rope(x, xs, gc, gs, slot_ones):
    ss = _dot((x * x).astype(BF16), slot_ones)
    return (x * gc + xs * gs) * lax.rsqrt(ss * (1.0 / MLA_QK) + EPS)


def _inproj_kernel(x_ref, mod_ref, w_ref, lbl_ref, qag_ref, wuq_ref, kvag_ref, wuk_ref, wuv_ref,
                   qg_ref, qgs_ref, kg_ref, kgs_ref, cos_ref, sin_ref,
                   hq_ref, hv_ref, lff_ref, kf_ref, lfb_ref, kb_ref, g_ref, pu_ref,
                   q_ref, k_ref, v_ref, *, layer):
    x = x_ref[...]
    h = _mod_norm(x, mod_ref[3:4, :], mod_ref[4:5, :]).astype(BF16)
    y = _dot(h, w_ref[...])

    logits = lbl_ref[...]
    e = jnp.exp(logits - jnp.max(logits, axis=0, keepdims=True))
    p = e / jnp.sum(e, axis=0, keepdims=True)
    lb = jnp.sum(p[:layer + 1], axis=0) - p[0]

    hq_ref[...] = _silu(y[:, C_Q:C_Q + HG_W])
    hv_ref[...] = y[:, C_I:C_I + HG_W]
    g_ref[...] = _silu(y[:, C_G:C_G + HG_W])
    for col, lrow, lf_ref, kk_ref in ((C_FF, 0, lff_ref, kf_ref), (C_FB, 1, lfb_ref, kb_ref)):
        lbv = lb[lrow:lrow + 1, :]
        f = lbv + (1.0 - lbv) * _sigmoid(y[:, col:col + HG_W])
        lf_ref[...] = jnp.log(jnp.clip(f, F_MIN, 1.0))
        kk_ref[...] = 1.0 - f
    pu_ref[...] = y[:, C_POOL:C_POOL + POOL_W]

    wide = MLA_HEADS * HEAD_PAD
    cq = y[:, C_CQ:C_CQ + MLA_Q_RANK]
    cqn = cq * lax.rsqrt(jnp.mean(cq * cq, axis=-1, keepdims=True) + EPS) * qag_ref[...]
    qf = _dot(cqn.astype(BF16), wuq_ref[...])
    ckv = y[:, C_CKV:C_CKV + MLA_KV_RANK]
    ckvn = (ckv * lax.rsqrt(jnp.mean(ckv * ckv, axis=-1, keepdims=True) + EPS) * kvag_ref[...]).astype(BF16)
    kn = _dot(ckvn, wuk_ref[...])
    v_ref[...] = _dot(ckvn, wuv_ref[...]).astype(BF16)
    kpe = y[:, C_KPE:C_KPE + HEAD_PAD]
    kpe2 = jnp.concatenate([kpe, kpe], axis=1)
    kpes = y[:, C_KPES:C_KPES + HEAD_PAD]
    kpes2 = jnp.concatenate([kpes, kpes], axis=1)

    def pair_tables(g_ref, gs_ref, scale):
        gc = g_ref[...] * scale * cos_ref[...]
        gs = gs_ref[...] * scale * sin_ref[...]
        return jnp.concatenate([gc, gc], axis=1), jnp.concatenate([gs, gs], axis=1)

    q_gc, q_gs = pair_tables(qg_ref, qgs_ref, MLA_QK ** -0.5 * LOG2_E)
    k_gc, k_gs = pair_tables(kg_ref, kgs_ref, 1.0)
    slot_ones = _same_group(2 * HEAD_PAD, 2 * HEAD_PAD, HEAD_PAD, HEAD_PAD).astype(BF16)
    for pr in range(MLA_HEADS // 2):
        sl = slice(pr * 2 * HEAD_PAD, (pr + 1) * 2 * HEAD_PAD)
        sls = slice(wide + pr * 2 * HEAD_PAD, wide + (pr + 1) * 2 * HEAD_PAD)
        q_ref[:, sl] = _pair_norm_rope(qf[:, sl], qf[:, sls], q_gc, q_gs, slot_ones).astype(BF16)
        kh = _pair_norm_rope(kn[:, sl] + kpe2, kpes2, k_gc, k_gs, slot_ones)
        k_ref[sl, :] = kh.T.astype(BF16)


def _inproj(xs, mod, wts, layer, dims):
    n_batch, seq, ctx, n_lat, n_tokens, d, depth = dims
    tm = TOKEN_TILE
    lat_tiles, per_b = n_lat // tm, seq // tm
    s_all = seq + ctx

    def b_of(i):
        return jnp.where(i < lat_tiles, i // per_b, i - lat_tiles)

    def r_of(i):
        return jnp.where(i < lat_tiles, i % per_b, per_b)

    def const(*shape):
        return pl.BlockSpec(shape, lambda i: (0,) * len(shape))

    def per_layer(*shape):
        return pl.BlockSpec((None,) + shape, lambda i: (layer,) + (0,) * len(shape))

    def seq_out(width):
        return pl.BlockSpec((None, tm, width), lambda i: (b_of(i), r_of(i), 0))

    rope_spec = pl.BlockSpec((tm, HEAD_PAD), lambda i: (r_of(i), 0))
    f32_out = jax.ShapeDtypeStruct((n_batch, s_all, HG_W), F32)
    wide = MLA_HEADS * HEAD_PAD
    return pl.pallas_call(
        functools.partial(_inproj_kernel, layer=layer),
        grid=(n_tokens // tm,),
        in_specs=[
            pl.BlockSpec((tm, d), lambda i: (i, 0)),
            pl.BlockSpec((None, None, N_MOD, d), lambda i: (layer, jnp.where(i < lat_tiles, i // per_b, n_batch), 0, 0)),
            per_layer(d, IN_PAD),
            const(depth, 2, HG_W),
            per_layer(1, MLA_Q_RANK), per_layer(MLA_Q_RANK, 2 * wide),
            per_layer(1, MLA_KV_RANK), per_layer(MLA_KV_RANK, wide), per_layer(MLA_KV_RANK, MLA_HEADS * MLA_V),
            per_layer(1, HEAD_PAD), per_layer(1, HEAD_PAD), per_layer(1, HEAD_PAD), per_layer(1, HEAD_PAD),
            rope_spec, rope_spec,
        ],
        out_specs=[seq_out(HG_W)] * 8 + [
            seq_out(wide), pl.BlockSpec((None, wide, tm), lambda i: (b_of(i), 0, r_of(i))),
            seq_out(MLA_HEADS * MLA_V)],
        out_shape=[f32_out] * 8 + [
            jax.ShapeDtypeStruct((n_batch, s_all, wide), BF16),
            jax.ShapeDtypeStruct((n_batch, wide, s_all), BF16),
            jax.ShapeDtypeStruct((n_batch, s_all, MLA_HEADS * MLA_V), BF16)],
        compiler_params=pltpu.CompilerParams(
            dimension_semantics=("arbitrary",), vmem_limit_bytes=VMEM_LIMIT),
        name="mixer_in_projection",
    )(xs, mod, wts["w_in"], wts["lb_logits"], wts["q_a_gain"], wts["w_uq"], wts["kv_a_gain"],
      wts["w_uk"], wts["w_uv"], wts["q_gain"], wts["q_gain_s"], wts["k_gain"], wts["k_gain_s"],
      wts["cos"], wts["sin"])


def _hg_chunk(q, k, v, lf, st, rev):
    c, sub = HG_CHUNK, HG_SUB
    n_sub = c // sub
    ri = lax.broadcasted_iota(jnp.int32, (c, c), 0)
    ci = lax.broadcasted_iota(jnp.int32, (c, c), 1)
    tri = ((ci >= ri) if rev else (ci <= ri)).astype(BF16)
    b = _exact_dot_left(tri, lf) * LOG2_E
    b_tot = b[0:1, :] if rev else b[c - 1:c, :]
    head_sq = _same_group(HG_W, HG_W, HG_DV, HG_DK)
    head_sub = _same_group(n_sub * sub, HG_W, sub, HG_DK)
    block_ones = head_sq.astype(BF16)

    o = _dot_nt((q * jnp.exp2(b)).astype(BF16), st.astype(BF16))
    kd = (k * jnp.exp2(b_tot - b)).astype(BF16)
    upd = _dot_tn(v.astype(BF16), kd)
    st_new = st * jnp.exp2(b_tot) + jnp.where(head_sq, upd, 0.0)

    def rows(a, i):
        return a[i * sub:(i + 1) * sub, :]

    parts = [[rows(o, i)] for i in range(n_sub)]
    for j in (range(1, n_sub) if rev else range(n_sub - 1)):
        kj, bj, vj = rows(k, j), rows(b, j), rows(v, j)
        ref = bj[0:1, :] if rev else bj[sub - 1:sub, :]
        kh = kj * jnp.exp2(ref - bj)
        kh4 = jnp.where(head_sub, jnp.concatenate([kh] * HG_HEADS, axis=0), 0.0).astype(BF16)
        v4 = jnp.where(head_sub, jnp.concatenate([vj] * HG_HEADS, axis=0), 0.0).astype(BF16)
        lo, hi = (0, j * sub) if rev else ((j + 1) * sub, c)
        qt = (q[lo:hi, :] * jnp.exp2(b[lo:hi, :] - ref)).astype(BF16)
        att = _dot_nt(qt, kh4)
        oa = _dot(att.astype(BF16), v4)
        for i in range(lo // sub, hi // sub):
            parts[i].append(oa[i * sub - lo:(i + 1) * sub - lo, :])
    g = sub // 2
    cidx = lax.broadcasted_iota(jnp.int32, (sub, HG_W), 0)
    full_keys = range(g, sub) if rev else range(g)
    half_keys = range(g) if rev else range(g, sub)
    hrows = slice(0, g) if rev else slice(g, sub)
    cidx_half = lax.broadcasted_iota(jnp.int32, (g, HG_W), 0) + hrows.start

    def pair_terms(qq, bb, cc, ks, bs, s):
        keep = (cc <= s) if rev else (cc >= s)
        return qq * ks * jnp.exp2(jnp.where(keep, bb - bs, MASKED_LOG2))

    pieces = []
    for i in range(n_sub):
        qi, ki, bi = rows(q, i), rows(k, i), rows(b, i)
        for s in full_keys:
            pieces.append(pair_terms(qi, bi, cidx, ki[s:s + 1, :], bi[s:s + 1, :], s).astype(BF16))
        halves = [pair_terms(qi[hrows, :], bi[hrows, :], cidx_half, ki[s:s + 1, :], bi[s:s + 1, :], s)
                  for s in half_keys]
        for h0, h1 in zip(halves[0::2], halves[1::2]):
            pieces.append(jnp.concatenate([h0, h1], axis=0).astype(BF16))
    r = _dot(jnp.concatenate(pieces, axis=0), block_ones)
    off = 0
    for i in range(n_sub):
        vi = rows(v, i)
        acc_full = None
        for s in full_keys:
            t = r[off:off + sub, :] * vi[s:s + 1, :]
            acc_full = t if acc_full is None else acc_full + t
            off += sub
        acc_half = None
        for s0, s1 in zip(half_keys[0::2], half_keys[1::2]):
            t = r[off:off + g, :] * vi[s0:s0 + 1, :] + r[off + g:off + sub, :] * vi[s1:s1 + 1, :]
            acc_half = t if acc_half is None else acc_half + t
            off += sub
        lo_half, hi_half = acc_full[:g, :], acc_full[g:, :]
        if rev:
            lo_half = lo_half + acc_half
        else:
            hi_half = hi_half + acc_half
        parts[i].append(jnp.concatenate([lo_half, hi_half], axis=0))
    outs = []
    for plist in parts:
        tot = plist[0]
        for t in plist[1:]:
            tot = tot + t
        outs.append(tot)
    return jnp.concatenate(outs, axis=0), st_new


def _hgrn_kernel(qf_ref, vf_ref, lff_ref, kf_ref, qb_ref, vb_ref, lfb_ref, kb_ref,
                 of_ref, ob_ref, sf_ref, sb_ref, *, tb):
    @pl.when(pl.program_id(1) == 0)
    def _():
        sf_ref[...] = jnp.zeros_like(sf_ref)
        sb_ref[...] = jnp.zeros_like(sb_ref)

    n_chunks = tb // HG_CHUNK
    n_seq = qf_ref.shape[0]
    sf = [sf_ref[j] for j in range(n_seq)]
    sb = [sb_ref[j] for j in range(n_seq)]
    for ci in range(n_chunks):
        sl = slice(ci * HG_CHUNK, (ci + 1) * HG_CHUNK)
        sr = slice((n_chunks - 1 - ci) * HG_CHUNK, (n_chunks - ci) * HG_CHUNK)
        for j in range(n_seq):
            of_ref[j, sl, :], sf[j] = _hg_chunk(qf_ref[j, sl, :], kf_ref[j, sl, :], vf_ref[j, sl, :],
                                                 lff_ref[j, sl, :], sf[j], False)
            ob_ref[j, sr, :], sb[j] = _hg_chunk(qb_ref[j, sr, :], kb_ref[j, sr, :], vb_ref[j, sr, :],
                                                 lfb_ref[j, sr, :], sb[j], True)
    for j in range(n_seq):
        sf_ref[j] = sf[j]
        sb_ref[j] = sb[j]


def _hgrn(hq, hv, lff, kf, lfb, kb, dims):
    n_batch, seq, ctx = dims[0], dims[1], dims[2]
    tb = TOKEN_TILE
    assert ctx == tb and seq % tb == 0
    per_b = seq // tb
    n_seq = math.gcd(n_batch, HG_SEQS)
    fwd = pl.BlockSpec((n_seq, tb, HG_W), lambda b, n: (b, jnp.where(n == 0, per_b, n - 1), 0))
    bwd = pl.BlockSpec((n_seq, tb, HG_W), lambda b, n: (b, jnp.where(n == 0, per_b, per_b - n), 0))
    shape = jax.ShapeDtypeStruct(hq.shape, F32)
    return pl.pallas_call(
        functools.partial(_hgrn_kernel, tb=tb),
        grid=(n_batch // n_seq, per_b + 1),
        in_specs=[fwd, fwd, fwd, fwd, bwd, bwd, bwd, bwd],
        out_specs=[fwd, bwd],
        out_shape=[shape, shape],
        scratch_shapes=[pltpu.VMEM((n_seq, HG_W, HG_W), F32), pltpu.VMEM((n_seq, HG_W, HG_W), F32)],
        compiler_params=pltpu.CompilerParams(
            dimension_semantics=("arbitrary", "arbitrary"), vmem_limit_bytes=VMEM_LIMIT),
        name="hgrn2_bidirectional_scan",
    )(hq, hv, lff, kf, hq, hv, lfb, kb)


def _attn_kernel(q_ref, kt_ref, v_ref, *rest):
    o_ref = rest[-1]
    tq = q_ref.shape[0]
    sub = min(ATTN_SUB, tq)
    lane = lax.broadcasted_iota(jnp.int32, (sub, 2 * MLA_V), 1)
    for t in range(tq // sub):
        rows = slice(t * sub, (t + 1) * sub)
        outs = []
        for hh in range(2):
            hs = slice(hh * HEAD_PAD, (hh + 1) * HEAD_PAD)
            s = _dot(q_ref[rows, hs], kt_ref[hs, :])
            p = jnp.exp2(s - jnp.max(s, axis=-1, keepdims=True))
            l = jnp.sum(p, axis=-1, keepdims=True)
            outs.append(_dot(p.astype(BF16), v_ref[...]) / l)
        o_ref[rows, :] = jnp.where(lane < MLA_V, outs[0], outs[1]).astype(o_ref.dtype)


def _attention(q, kt, v, dims, with_ctx):
    n_batch, seq, ctx = dims[0], dims[1], dims[2]
    s_all = seq + ctx
    tq = min(ATTN_TILE, seq)
    assert seq % tq == 0 and seq % ctx == 0
    pairs = MLA_HEADS // 2
    out_shape = jax.ShapeDtypeStruct((n_batch, s_all, MLA_HEADS * MLA_V), BF16)
    params = pltpu.CompilerParams(
        dimension_semantics=("arbitrary", "arbitrary", "arbitrary"), vmem_limit_bytes=VMEM_LIMIT)
    att = pl.pallas_call(
        _attn_kernel,
        grid=(n_batch, pairs, seq // tq),
        in_specs=[
            pl.BlockSpec((None, tq, 2 * HEAD_PAD), lambda b, p, i: (b, i, p)),
            pl.BlockSpec((None, 2 * HEAD_PAD, s_all), lambda b, p, i: (b, p, 0)),
            pl.BlockSpec((None, s_all, 2 * MLA_V), lambda b, p, i: (b, 0, p)),
        ],
        out_specs=pl.BlockSpec((None, tq, 2 * MLA_V), lambda b, p, i: (b, i, p)),
        out_shape=out_shape,
        compiler_params=params,
        name="latent_attention",
    )(q, kt, v)
    if not with_ctx:
        return att
    cb = seq // ctx
    return pl.pallas_call(
        _attn_kernel,
        grid=(n_batch, pairs, 1),
        in_specs=[
            pl.BlockSpec((None, ctx, 2 * HEAD_PAD), lambda b, p, i: (b, cb, p)),
            pl.BlockSpec((None, 2 * HEAD_PAD, ctx), lambda b, p, i: (b, p, cb)),
            pl.BlockSpec((None, ctx, 2 * MLA_V), lambda b, p, i: (b, cb, p)),
            pl.BlockSpec(memory_space=pl.ANY),
        ],
        out_specs=pl.BlockSpec((None, ctx, 2 * MLA_V), lambda b, p, i: (b, cb, p)),
        out_shape=out_shape,
        input_output_aliases={3: 0},
        compiler_params=params,
        name="context_attention",
    )(q, kt, v, att)


def _outproj_kernel(x_ref, mod_ref, of_ref, ob_ref, g_ref, att_ref, pu_ref, pprev_ref, pnext_ref,
                    hgain_ref, wp_ref, pscale_ref, wout_ref, o_ref, *, lat_tiles, per_b, seq, ctx):
    i = pl.program_id(0)
    tm = x_ref.shape[0]
    is_lat = i < lat_tiles
    r = jnp.where(is_lat, i % per_b, 0)
    seq_len = jnp.where(is_lat, seq, ctx)
    seq_tiles = jnp.where(is_lat, per_b, ctx // tm)

    o = of_ref[...] + ob_ref[...]
    block_ones = _same_group(HG_W, HG_W, HG_DV, HG_DV).astype(BF16)
    ms = _exact_dot_right(o * o, block_ones) * (1.0 / HG_DV)
    hg = o * lax.rsqrt(ms + EPS) * hgain_ref[...] * g_ref[...]

    h8 = POOL_HALO
    prev = jnp.where(r > 0, pprev_ref[...], 0.0)
    nxt = jnp.where(r < seq_tiles - 1, pnext_ref[...], 0.0)
    u = pu_ref[...]
    ext = jnp.concatenate([prev, u, nxt], axis=0)
    n_ext = tm + 2 * h8

    def shifted(a, delta):
        return pltpu.roll(a, (-delta) % n_ext, 0)

    s2 = ext + shifted(ext, -1)
    s4 = shifted(s2, -1) + shifted(s2, 1)
    s8 = shifted(s4, -2) + shifted(s4, 2)
    s16 = shifted(s8, -4) + shifted(s8, 4)
    pos = r * tm + lax.broadcasted_iota(jnp.int32, (tm, POOL_W), 0)
    lane_g = lax.broadcasted_iota(jnp.int32, (tm, POOL_W), 1) // POOL_CH
    def by_group(values):
        out = values[-1]
        for g in range(len(values) - 2, -1, -1):
            out = jnp.where(lane_g == g, values[g], out)
        return out

    back = by_group([w // 2 for w in POOL_WINDOWS])
    ahead = by_group([w - 1 - w // 2 for w in POOL_WINDOWS])
    cnt = jnp.minimum(pos + ahead, seq_len - 1) - jnp.maximum(pos - back, 0) + 1
    sums = by_group([a[h8:h8 + tm, :] for a in (s2, s4, s8, s16)])
    pooled = sums / cnt.astype(F32) - u
    pool = _dot(pooled.astype(BF16), wp_ref[...]) * pscale_ref[...]

    mix = jnp.concatenate([hg.astype(BF16), att_ref[...], pool.astype(BF16)], axis=1)
    o_ref[...] = x_ref[...] + mod_ref[5:6, :] * _dot(mix, wout_ref[...])


def _outproj(xs, mod, o_f, o_b, g, att, pu, wts, layer, dims, with_ctx):
    n_batch, seq, ctx, n_lat, n_tokens, d, depth = dims
    tm = TOKEN_TILE
    lat_tiles, per_b = n_lat // tm, seq // tm
    n_out = n_tokens if with_ctx else n_lat
    s_all = seq + ctx
    hb = tm // POOL_HALO

    def b_of(i):
        return jnp.where(i < lat_tiles, i // per_b, i - lat_tiles)

    def r_of(i):
        return jnp.where(i < lat_tiles, i % per_b, per_b)

    def seq_in(width):
        return pl.BlockSpec((None, tm, width), lambda i: (b_of(i), r_of(i), 0))

    def per_layer(*shape):
        return pl.BlockSpec((None,) + shape, lambda i: (layer,) + (0,) * len(shape))

    last_halo = s_all // POOL_HALO - 1
    prev_spec = pl.BlockSpec((None, POOL_HALO, POOL_W),
                             lambda i: (b_of(i), jnp.maximum(r_of(i) * hb - 1, 0), 0))
    next_spec = pl.BlockSpec((None, POOL_HALO, POOL_W),
                             lambda i: (b_of(i), jnp.minimum((r_of(i) + 1) * hb, last_halo), 0))
    return pl.pallas_call(
        functools.partial(_outproj_kernel, lat_tiles=lat_tiles, per_b=per_b, seq=seq, ctx=ctx),
        grid=(n_out // tm,),
        in_specs=[
            pl.BlockSpec((tm, d), lambda i: (i, 0)),
            pl.BlockSpec((None, None, N_MOD, d), lambda i: (layer, jnp.where(i < lat_tiles, i // per_b, n_batch), 0, 0)),
            seq_in(HG_W), seq_in(HG_W), seq_in(HG_W), seq_in(MLA_HEADS * MLA_V), seq_in(POOL_W),
            prev_spec, next_spec,
            per_layer(1, HG_W), per_layer(POOL_W, POOL_W), per_layer(1, POOL_W), per_layer(d, d),
        ],
        out_specs=pl.BlockSpec((tm, d), lambda i: (i, 0)),
        out_shape=jax.ShapeDtypeStruct((n_out, d), F32),
        compiler_params=pltpu.CompilerParams(
            dimension_semantics=("arbitrary",), vmem_limit_bytes=VMEM_LIMIT),
        name="mixer_out_projection",
    )(xs, mod, o_f, o_b, g, att, pu, pu, pu, wts["hg_gain"], wts["w_pool"], wts["pool_scale"], wts["w_out"])


def _rope_tables(seq, ctx):
    rows = seq // GRID_W
    row = np.repeat(np.arange(rows), GRID_W).astype(np.float32)
    col = np.tile(np.arange(GRID_W), rows).astype(np.float32)
    n_freq = MLA_ROPE // 4
    inv_freq = jnp.asarray(ROPE_BASE, F32) ** (-jnp.arange(n_freq, dtype=F32) / n_freq)
    ang = jnp.concatenate([jnp.asarray(row)[:, None] * inv_freq, jnp.asarray(col)[:, None] * inv_freq], axis=-1)
    cos, sin = jnp.cos(ang), jnp.sin(ang)
    half = MLA_ROPE // 2
    zeros = lambda w: jnp.zeros((seq, w), F32)
    cos_t = jnp.concatenate([jnp.ones((seq, MLA_NOPE), F32), cos, cos, zeros(HEAD_PAD - MLA_QK)], axis=1)
    sin_s = jnp.concatenate([zeros(MLA_NOPE), -sin, sin, zeros(HEAD_PAD - MLA_QK)], axis=1)
    ident = jnp.concatenate([jnp.ones((ctx, MLA_QK), F32), jnp.zeros((ctx, HEAD_PAD - MLA_QK), F32)], axis=1)
    return (jnp.concatenate([cos_t, ident], axis=0),
            jnp.concatenate([sin_s, jnp.zeros((ctx, HEAD_PAD), F32)], axis=0))


def _prepare_weights(p, dims):
    seq, ctx_len, d, depth = dims[1], dims[2], dims[5], dims[6]
    pad_cols = lambda a, n: jnp.pad(a, [(0, 0)] * (a.ndim - 1) + [(0, n)])
    w_in = p["w_in"]
    half = MLA_ROPE // 2

    def slot(a):
        return pad_cols(a, HEAD_PAD - MLA_QK)

    def swapped_slot(a):
        return jnp.concatenate([jnp.zeros_like(a[..., :MLA_NOPE]), a[..., MLA_NOPE + half:],
                                a[..., MLA_NOPE:MLA_NOPE + half], jnp.zeros_like(a[..., :HEAD_PAD - MLA_QK])], axis=-1)

    kpe_cols = jnp.concatenate([jnp.zeros((depth, d, MLA_NOPE), F32), w_in[:, :, C_POOL:C_POOL + MLA_ROPE]], axis=-1)
    w_in_p = jnp.concatenate([
        w_in[:, :, :C_POOL], w_in[:, :, C_POOL + MLA_ROPE:], slot(kpe_cols), swapped_slot(kpe_cols)],
        axis=-1).astype(BF16)
    w_uq4 = p["mla_w_uq"].reshape(depth, MLA_Q_RANK, MLA_HEADS, MLA_QK)
    wide = MLA_HEADS * HEAD_PAD
    w_uq = jnp.concatenate([slot(w_uq4).reshape(depth, MLA_Q_RANK, wide),
                            swapped_slot(w_uq4).reshape(depth, MLA_Q_RANK, wide)], axis=-1)
    w_ukv = p["mla_w_ukv"].reshape(depth, MLA_KV_RANK, MLA_HEADS, MLA_NOPE + MLA_V)
    cos_t, sin_s = _rope_tables(seq, ctx_len)
    eye = jnp.eye(len(POOL_WINDOWS), dtype=F32)
    wts = {
        "w_in": w_in_p,
        "lb_logits": p["hg_lb_logits"].astype(F32),
        "q_a_gain": p["mla_q_a_gain"].reshape(depth, 1, MLA_Q_RANK),
        "w_uq": w_uq.astype(BF16),
        "kv_a_gain": p["mla_kv_a_gain"].reshape(depth, 1, MLA_KV_RANK),
        "w_uk": pad_cols(w_ukv[..., :MLA_NOPE], HEAD_PAD - MLA_NOPE).reshape(
            depth, MLA_KV_RANK, MLA_HEADS * HEAD_PAD).astype(BF16),
        "w_uv": w_ukv[..., MLA_NOPE:].reshape(depth, MLA_KV_RANK, MLA_HEADS * MLA_V).astype(BF16),
        "q_gain": slot(p["mla_q_gain"]).reshape(depth, 1, HEAD_PAD),
        "q_gain_s": swapped_slot(p["mla_q_gain"]).reshape(depth, 1, HEAD_PAD),
        "k_gain": slot(p["mla_k_gain"]).reshape(depth, 1, HEAD_PAD),
        "k_gain_s": swapped_slot(p["mla_k_gain"]).reshape(depth, 1, HEAD_PAD),
        "cos": cos_t, "sin": sin_s,
        "hg_gain": jnp.tile(p["hg_out_gain"], (1, HG_HEADS)).reshape(depth, 1, HG_W),
        "w_pool": jnp.einsum("gh,lgcd->lgchd", eye, p["pool_w"]).reshape(depth, POOL_W, POOL_W).astype(BF16),
        "pool_scale": p["pool_scale"].reshape(depth, 1, POOL_W),
        "w_out": p["w_out"].astype(BF16),
    }
    ffn = tuple(p[n].astype(BF16) for n in ("ffn1_w_in", "ffn1_w_out", "ffn2_w_in", "ffn2_w_out"))
    return wts, ffn


def kernel(x, c, ctx, c_ctx, w_mod, b_mod, ffn1_w_in, ffn1_w_out, w_in, w_out, hg_lb_logits,
           hg_out_gain, mla_q_a_gain, mla_w_uq, mla_kv_a_gain, mla_w_ukv, mla_q_gain, mla_k_gain,
           pool_w, pool_scale, ffn2_w_in, ffn2_w_out):
    n_batch, seq, d = x.shape
    ctx_len = ctx.shape[1]
    depth = w_mod.shape[0]
    n_lat, n_ctx = n_batch * seq, n_batch * ctx_len
    n_tokens = n_lat + n_ctx
    dims = (n_batch, seq, ctx_len, n_lat, n_tokens, d, depth)
    assert n_batch + 1 <= SUBLANES and seq % FFN_TILE == 0 and n_ctx % FFN_TILE == 0
    wts, (f1_in, f1_out, f2_in, f2_out) = _prepare_weights(dict(
        w_in=w_in, w_out=w_out, hg_lb_logits=hg_lb_logits, hg_out_gain=hg_out_gain,
        mla_q_a_gain=mla_q_a_gain, mla_w_uq=mla_w_uq, mla_kv_a_gain=mla_kv_a_gain, mla_w_ukv=mla_w_ukv,
        mla_q_gain=mla_q_gain, mla_k_gain=mla_k_gain, pool_w=pool_w, pool_scale=pool_scale,
        ffn1_w_in=ffn1_w_in, ffn1_w_out=ffn1_w_out, ffn2_w_in=ffn2_w_in, ffn2_w_out=ffn2_w_out), dims)

    cond = jnp.concatenate([c, c_ctx[None, :], jnp.zeros((SUBLANES - n_batch - 1, d), F32)], axis=0)
    mod = _modulation(cond, w_mod, b_mod).reshape(depth, SUBLANES, N_MOD, d)

    xs = jnp.concatenate([x.reshape(n_lat, d), ctx.reshape(n_ctx, d)], axis=0)
    for l in range(depth):
        last = l == depth - 1
        xs = _ffn(xs, mod, f1_in, f1_out, l, 0, n_tokens, n_lat, seq, n_batch)
        hq, hv, lff, kf, lfb, kb, g, pu, q, k, v = _inproj(xs, mod, wts, l, dims)
        o_f, o_b = _hgrn(hq, hv, lff, kf, lfb, kb, dims)
        att = _attention(q, k, v, dims, with_ctx=not last)
        xs = _outproj(xs, mod, o_f, o_b, g, att, pu, wts, l, dims, with_ctx=not last)
        xs = _ffn(xs, mod, f2_in, f2_out, l, 6, n_lat if last else n_tokens, n_lat, seq, n_batch)
    return xs.reshape(n_batch, seq, d)
```
